```python
import math
import jax, jax.numpy as jnp
from jax import lax
import numpy as np

D_MODEL = 1024
BATCH = 2
SEQ = 8192
DEPTH = 1
DEC_BATCH = 4
DEC_SEQ = 8192
PAST_LEN = 128

D_MIX = D_MODEL
A_HEADS = 4
A_DIM = 64
A_VDIM = 2 * A_DIM
B_HEADS = 8
B_KV_HEADS = 2
B_DIM = 64
B_GROUP = B_HEADS // B_KV_HEADS
A_Q_COLS = A_HEADS * 2 * A_DIM
A_K_COLS = A_HEADS * 2 * A_DIM
A_V_COLS = A_HEADS * A_VDIM
B_Q_COLS = B_HEADS * B_DIM
B_KV_COLS = B_KV_HEADS * B_DIM
OFF_AK = A_Q_COLS
OFF_AV = OFF_AK + A_K_COLS
OFF_BQ = OFF_AV + A_V_COLS
OFF_BK = OFF_BQ + B_Q_COLS
OFF_BV = OFF_BK + B_KV_COLS
IN_COLS = OFF_BV + B_KV_COLS
A_OUT = A_HEADS * A_VDIM
B_OUT = B_HEADS * B_DIM
ROPE_THETA = 10000.0
GRID_W = 64
Q_BLOCK = 128
NORM_EPS = 1e-6
PEER_HEADS = 8
PEER_N_KEYS = 128
PEER_N_EXPERTS = PEER_N_KEYS * PEER_N_KEYS
PEER_D_QUERY = 256
PEER_D_HALF = PEER_D_QUERY // 2
PEER_TOPK = 16
PEER_BLOCK = 128

kernel_name = "hymba_diffattn_axialgqa_peer_encoder"


def rms_norm(x, w):
    xf = x.astype(jnp.float32)
    xf = xf * lax.rsqrt(jnp.mean(xf * xf, axis=-1, keepdims=True) + NORM_EPS)
    return (xf * w.astype(jnp.float32)).astype(x.dtype)


def rope_tables(pos, dim):
    inv = ROPE_THETA ** (-jnp.arange(0, dim, 2, dtype=jnp.float32) / dim)
    ang = pos[:, None] * inv[None, :]
    ang = jnp.concatenate([ang, ang], axis=-1)
    return jnp.cos(ang), jnp.sin(ang)


def apply_rope(x, cos, sin):
    half = x.shape[-1] // 2
    xf = x.astype(jnp.float32)
    rot = jnp.concatenate([-xf[..., half:], xf[..., :half]], axis=-1)
    return (xf * cos[None, :, None, :] + rot * sin[None, :, None, :]).astype(x.dtype)


def apply_axial_rope(x, cos_r, sin_r, cos_c, sin_c):
    half = x.shape[-1] // 2
    return jnp.concatenate([apply_rope(x[..., :half], cos_r, sin_r),
                            apply_rope(x[..., half:], cos_c, sin_c)], axis=-1)


def to_qblocks(t):
    b, s = t.shape[:2]
    return jnp.moveaxis(t.reshape(b, s // Q_BLOCK, Q_BLOCK, *t.shape[2:]), 1, 0)


def from_qblocks(o):
    nb, b, q = o.shape[:3]
    return jnp.moveaxis(o, 0, 1).reshape(b, nb * q, *o.shape[3:])


def diff_attention(q1, q2, k1, k2, v, lam):
    scale = A_DIM ** -0.5

    def block(qs):
        qb1, qb2 = qs
        a1 = jax.nn.softmax(jnp.einsum('bqhd,bkhd->bhqk', qb1, k1).astype(jnp.float32) * scale, axis=-1)
        a2 = jax.nn.softmax(jnp.einsum('bqhd,bkhd->bhqk', qb2, k2).astype(jnp.float32) * scale, axis=-1)
        attn = (a1 - lam * a2).astype(v.dtype)
        return jnp.einsum('bhqk,bkhe->bqhe', attn, v)

    return from_qblocks(lax.map(block, (to_qblocks(q1), to_qblocks(q2))))


def gqa_attention(q, k, v):
    b, s = q.shape[:2]
    scale = B_DIM ** -0.5
    qg = q.reshape(b, s, B_KV_HEADS, B_GROUP, B_DIM)

    def block(qb):
        p = jax.nn.softmax(jnp.einsum('bqgrd,bkgd->bgrqk', qb, k).astype(jnp.float32) * scale, axis=-1)
        return jnp.einsum('bgrqk,bkgd->bqgrd', p.astype(v.dtype), v)

    out = from_qblocks(lax.map(block, to_qblocks(qg)))
    return out.reshape(b, s, B_OUT)


def peer_ffn(xn, wq, keys, u, v):
    b, s, d = xn.shape
    xt = xn.reshape((b * s) // PEER_BLOCK, PEER_BLOCK, d)

    def block(xb):
        q = (xb @ wq).reshape(PEER_BLOCK, PEER_HEADS, 2, PEER_D_HALF)
        sc = jnp.einsum('thcd,hcnd->thcn', q, keys).astype(jnp.float32)
        s1, i1 = lax.top_k(sc[:, :, 0], PEER_TOPK)
        s2, i2 = lax.top_k(sc[:, :, 1], PEER_TOPK)
        cand_s = (s1[..., :, None] + s2[..., None, :]).reshape(PEER_BLOCK, PEER_HEADS, PEER_TOPK * PEER_TOPK)
        cand_i = (i1[..., :, None] * PEER_N_KEYS + i2[..., None, :]).reshape(PEER_BLOCK, PEER_HEADS, PEER_TOPK * PEER_TOPK)
        top_s, pos = lax.top_k(cand_s, PEER_TOPK)
        idx = jnp.take_along_axis(cand_i, pos, axis=-1)
        g = jax.nn.softmax(top_s, axis=-1)
        act = jax.nn.gelu(jnp.einsum('thkd,td->thk', u[idx], xb).astype(jnp.float32), approximate=False)
        w = (g * act).astype(xb.dtype)
        return jnp.einsum('thk,thkd->td', w, v[idx])

    return lax.map(block, xt).reshape(b, s, d)


def encoder_layer(x, layer_idx, tables, attn_norm, w_in, lambda_q1, lambda_k1, lambda_q2, lambda_k2,
                  subln_w, q_norm_w, k_norm_w, w_out, ffn_norm, peer_wq, peer_keys, peer_u, peer_v):
    cos_a, sin_a, cos_r, sin_r, cos_c, sin_c = tables
    b, s, _ = x.shape
    lambda_init = 0.8 - 0.6 * math.exp(-0.3 * (layer_idx + 1))

    h = rms_norm(x, attn_norm)
    proj = h @ w_in
    qa = proj[..., :OFF_AK].reshape(b, s, A_HEADS, 2, A_DIM)
    ka = proj[..., OFF_AK:OFF_AV].reshape(b, s, A_HEADS, 2, A_DIM)
    va = proj[..., OFF_AV:OFF_BQ].reshape(b, s, A_HEADS, A_VDIM)
    qa1 = apply_rope(qa[..., 0, :], cos_a, sin_a)
    qa2 = apply_rope(qa[..., 1, :], cos_a, sin_a)
    ka1 = apply_rope(ka[..., 0, :], cos_a, sin_a)
    ka2 = apply_rope(ka[..., 1, :], cos_a, sin_a)
    lam = (jnp.exp(jnp.sum(lambda_q1.astype(jnp.float32) * lambda_k1.astype(jnp.float32)))
           - jnp.exp(jnp.sum(lambda_q2.astype(jnp.float32) * lambda_k2.astype(jnp.float32)))
           + lambda_init)
    out_a = diff_attention(qa1, qa2, ka1, ka2, va, lam)
    out_a = (rms_norm(out_a, subln_w) * (1.0 - lambda_init)).reshape(b, s, A_OUT)
    qb = proj[..., OFF_BQ:OFF_BK].reshape(b, s, B_HEADS, B_DIM)
    kb = proj[..., OFF_BK:OFF_BV].reshape(b, s, B_KV_HEADS, B_DIM)
    vb = proj[..., OFF_BV:].reshape(b, s, B_KV_HEADS, B_DIM)
    qb = apply_axial_rope(rms_norm(qb, q_norm_w), cos_r, sin_r, cos_c, sin_c)
    kb = apply_axial_rope(rms_norm(kb, k_norm_w), cos_r, sin_r, cos_c, sin_c)
    out_b = gqa_attention(qb, kb, vb)
    x = x + jnp.concatenate([out_a, out_b], axis=-1) @ w_out
    x = x + peer_ffn(rms_norm(x, ffn_norm), peer_wq, peer_keys, peer_u, peer_v)
    return x


def encoder_forward(x, attn_norm, w_in, lambda_q1, lambda_k1, lambda_q2, lambda_k2, subln_w,
                    q_norm_w, k_norm_w, w_out, ffn_norm, peer_wq, peer_keys, peer_u, peer_v, final_norm):
    s = x.shape[1]
    rows = s // GRID_W
    t = jnp.arange(s, dtype=jnp.float32)
    row = jnp.broadcast_to(jnp.arange(rows, dtype=jnp.float32)[:, None], (rows, GRID_W)).reshape(s)
    col = jnp.broadcast_to(jnp.arange(GRID_W, dtype=jnp.float32)[None, :], (rows, GRID_W)).reshape(s)
    cos_a, sin_a = rope_tables(t, A_DIM)
    cos_r, sin_r = rope_tables(row, B_DIM // 2)
    cos_c, sin_c = rope_tables(col, B_DIM // 2)
    tables = (cos_a, sin_a, cos_r, sin_r, cos_c, sin_c)
    for i in range(DEPTH):
        x = encoder_layer(x, i, tables, attn_norm[i], w_in[i], lambda_q1[i], lambda_k1[i], lambda_q2[i],
                          lambda_k2[i], subln_w[i], q_norm_w[i], k_norm_w[i], w_out[i], ffn_norm[i],
                          peer_wq[i], peer_keys[i], peer_u[i], peer_v[i])
    return rms_norm(x, final_norm)


def setup_inputs(seed: int = 0) -> dict:
    key = jax.random.key(seed)
    ks = jax.random.split(key, 19)
    f32 = jnp.float32

    def nrm(k, shape, scale):
        return jax.random.normal(k, shape, dtype=f32) * scale

    def gain(k, shape):
        return 1.0 + 0.01 * jax.random.normal(k, shape, dtype=f32)

    return {
        "x_prompt": nrm(ks[0], (BATCH, SEQ, D_MODEL), 1.0),
        "x_sample": nrm(ks[1], (DEC_BATCH, DEC_SEQ, D_MODEL), 1.0),
        "attn_norm": gain(ks[2], (DEPTH, D_MODEL)),
        "w_in": nrm(ks[3], (DEPTH, D_MODEL, IN_COLS), D_MODEL ** -0.5),
        "lambda_q1": nrm(ks[4], (DEPTH, A_DIM), 0.1),
        "lambda_k1": nrm(ks[5], (DEPTH, A_DIM), 0.1),
        "lambda_q2": nrm(ks[6], (DEPTH, A_DIM), 0.1),
        "lambda_k2": nrm(ks[7], (DEPTH, A_DIM), 0.1),
        "subln_w": gain(ks[8], (DEPTH, A_VDIM)),
        "q_norm_w": gain(ks[9], (DEPTH, B_DIM)),
        "k_norm_w": gain(ks[10], (DEPTH, B_DIM)),
        "w_out": nrm(ks[11], (DEPTH, D_MIX, D_MODEL), D_MIX ** -0.5),
        "ffn_norm": gain(ks[12], (DEPTH, D_MODEL)),
        "peer_wq": nrm(ks[13], (DEPTH, D_MODEL, PEER_HEADS * PEER_D_QUERY), D_MODEL ** -0.5),
        "peer_keys": nrm(ks[14], (DEPTH, PEER_HEADS, 2, PEER_N_KEYS, PEER_D_HALF), PEER_D_HALF ** -0.5),
        "peer_u": nrm(ks[15], (DEPTH, PEER_N_EXPERTS, D_MODEL), D_MODEL ** -0.5),
        "peer_v": nrm(ks[16], (DEPTH, PEER_N_EXPERTS, D_MODEL), (PEER_HEADS * PEER_TOPK) ** -0.5),
        "final_norm": gain(ks[17], (D_MODEL,)),
    }


def reference(x_prompt, x_sample, attn_norm, w_in, lambda_q1, lambda_k1, lambda_q2, lambda_k2, subln_w,
              q_norm_w, k_norm_w, w_out, ffn_norm, peer_wq, peer_keys, peer_u, peer_v, final_norm):
    y_prompt = encoder_forward(x_prompt, attn_norm, w_in, lambda_q1, lambda_k1, lambda_q2, lambda_k2, subln_w,
                               q_norm_w, k_norm_w, w_out, ffn_norm, peer_wq, peer_keys, peer_u, peer_v, final_norm)
    y_sample = encoder_forward(x_sample, attn_norm, w_in, lambda_q1, lambda_k1, lambda_q2, lambda_k2, subln_w,
                               q_norm_w, k_norm_w, w_out, ffn_norm, peer_wq, peer_keys, peer_u, peer_v, final_norm)
    return (y_prompt, y_sample)
```

```python
import functools
import math

import numpy as np
import jax
import jax.numpy as jnp
from jax import lax
from jax.experimental import pallas as pl
from jax.experimental.pallas import tpu as pltpu

F32 = jnp.float32
BF16 = jnp.bfloat16

D_MODEL = 1024
A_HEADS = 4
A_DIM = 64
A_VDIM = 2 * A_DIM
B_HEADS = 8
B_KV_HEADS = 2
B_DIM = 64
B_GROUP = B_HEADS // B_KV_HEADS
A_COLS = A_HEADS * 2 * A_DIM
B_Q_COLS = B_HEADS * B_DIM
B_KV_COLS = B_KV_HEADS * B_DIM
ROPE_THETA = 10000.0
GRID_W = 64
NORM_EPS = 1e-6
LAMBDA_INIT = 0.8 - 0.6 * math.exp(-0.3)
PEER_HEADS = 8
PEER_N_KEYS = 128
PEER_D_HALF = 128
PEER_TOPK = 16
PEER_PICKS = PEER_HEADS * PEER_TOPK

LANES = 128
HALF = LANES // 2
VMEM_LIMIT_BYTES = 56 * 1024 * 1024

OFF_QA, OFF_QA_ROT = 0, 512
OFF_KA, OFF_KA_ROT = 1024, 1536
OFF_VA = 2048
OFF_QB, OFF_QB_ROT = 2560, 3072
OFF_KB, OFF_KB_ROT = 3584, 3712
OFF_VB = 3840
FUSED_COLS = 3968


def _params(*semantics):
    return pltpu.CompilerParams(dimension_semantics=semantics, vmem_limit_bytes=VMEM_LIMIT_BYTES)


def _rope_tables(pos, dim):
    inv = ROPE_THETA ** (-jnp.arange(0, dim, 2, dtype=F32) / dim)
    ang = pos[:, None] * inv[None, :]
    ang = jnp.concatenate([ang, ang], axis=-1)
    return jnp.cos(ang), jnp.sin(ang)


def _position_tables(s):
    rows = s // GRID_W
    t = jnp.arange(s, dtype=F32)
    row = jnp.broadcast_to(jnp.arange(rows, dtype=F32)[:, None], (rows, GRID_W)).reshape(s)
    col = jnp.broadcast_to(jnp.arange(GRID_W, dtype=F32)[None, :], (rows, GRID_W)).reshape(s)
    cos_a, sin_a = _rope_tables(t, A_DIM)
    cos_r, sin_r = _rope_tables(row, B_DIM // 2)
    cos_c, sin_c = _rope_tables(col, B_DIM // 2)
    cos_b = jnp.concatenate([cos_r, cos_c], axis=-1)
    sin_b = jnp.concatenate([sin_r, sin_c], axis=-1)
    two = lambda a: jnp.concatenate([a, a], axis=-1)
    return two(cos_a), two(sin_a), two(cos_b), two(sin_b)


def _rotate_half_perm(n_cols, block):
    i = np.arange(n_cols)
    o = i % block
    half = block // 2
    perm = np.where(o < half, i + half, i - half)
    sign = np.where(o < half, -1.0, 1.0).astype(np.float32)
    return perm, sign


def _fused_in_weight(w_in):
    wqa = w_in[:, 0:512]
    wka = w_in[:, 512:1024]
    wva = w_in[:, 1024:1536]
    wqb = w_in[:, 1536:2048]
    wkb = w_in[:, 2048:2176]
    wvb = w_in[:, 2176:2304]

    def rot(m, block):
        perm, sign = _rotate_half_perm(m.shape[1], block)
        return m[:, perm] * sign

    parts = [wqa, rot(wqa, A_DIM), wka, rot(wka, A_DIM), wva,
             wqb, rot(wqb, B_DIM // 2), wkb, rot(wkb, B_DIM // 2), wvb]
    return jnp.concatenate(parts, axis=1).astype(BF16)


def _group_ones(n, group):
    i = np.arange(n) // group
    return jnp.asarray((i[:, None] == i[None, :]).astype(np.float32), dtype=BF16)


def _rms(x):
    return x * lax.rsqrt(jnp.mean(x * x, axis=-1, keepdims=True) + NORM_EPS)


def _group_mean_sq(p, ones):
    sq = p * p
    hi = sq.astype(BF16)
    lo = (sq - hi.astype(F32)).astype(BF16)
    ss = jnp.dot(hi, ones, preferred_element_type=F32) + jnp.dot(lo, ones, preferred_element_type=F32)
    return ss * (1.0 / B_DIM)


def _prep_kernel(x_ref, anw_ref, w_ref, cosa_ref, sina_ref, cosb_ref, sinb_ref,
                 qnw_ref, qnwp_ref, knw_ref, knwp_ref, ones_ref,
                 qa_ref, ka_ref, va_ref, qb_ref, kb_ref, vb_ref):
    x = x_ref[0]
    hb = (_rms(x) * anw_ref[...]).astype(BF16)

    def proj(off, n):
        return jnp.dot(hb, w_ref[:, off:off + n], preferred_element_type=F32)

    def tile_lanes(a, n):
        return jnp.concatenate([a] * n, axis=1)

    lane = lax.broadcasted_iota(jnp.int32, (x.shape[0], LANES), 1)
    low = lane < HALF

    cosa = tile_lanes(cosa_ref[...], A_COLS // LANES)
    sina = tile_lanes(sina_ref[...], A_COLS // LANES)
    qa = (proj(OFF_QA, A_COLS) * cosa + proj(OFF_QA_ROT, A_COLS) * sina) * (A_DIM ** -0.5)
    ka = proj(OFF_KA, A_COLS) * cosa + proj(OFF_KA_ROT, A_COLS) * sina
    zero = jnp.zeros_like(qa[:, :LANES])
    for h in range(A_HEADS):
        chunk = qa[:, h * LANES:(h + 1) * LANES]
        qa_ref[0, :, (2 * h) * LANES:(2 * h + 1) * LANES] = jnp.where(low, chunk, zero).astype(BF16)
        qa_ref[0, :, (2 * h + 1) * LANES:(2 * h + 2) * LANES] = jnp.where(low, zero, chunk).astype(BF16)
    ka_ref[0] = ka.astype(BF16)
    va_ref[0] = proj(OFF_VA, A_COLS).astype(BF16)

    cosb = tile_lanes(cosb_ref[...], B_Q_COLS // LANES)
    sinb = tile_lanes(sinb_ref[...], B_Q_COLS // LANES)
    pq = proj(OFF_QB, B_Q_COLS)
    rq = lax.rsqrt(_group_mean_sq(pq, ones_ref[...]) + NORM_EPS)
    qn = pq * rq * qnw_ref[...]
    qn_rot = proj(OFF_QB_ROT, B_Q_COLS) * rq * qnwp_ref[...]
    qb = (qn * cosb + qn_rot * sinb) * (B_DIM ** -0.5)
    for h in range(B_HEADS):
        chunk = qb[:, (h // 2) * LANES:(h // 2 + 1) * LANES]
        group = h // B_GROUP
        if h % 2 != group:
            chunk = pltpu.roll(chunk, HALF, axis=1)
        placed = jnp.where(low, chunk, zero) if group == 0 else jnp.where(low, zero, chunk)
        qb_ref[0, :, h * LANES:(h + 1) * LANES] = placed.astype(BF16)

    pk = proj(OFF_KB, B_KV_COLS)
    rk = lax.rsqrt(_group_mean_sq(pk, ones_ref[:B_KV_COLS, :B_KV_COLS]) + NORM_EPS)
    kn = pk * rk * knw_ref[...]
    kn_rot = proj(OFF_KB_ROT, B_KV_COLS) * rk * knwp_ref[...]
    kb_ref[0] = (kn * cosb_ref[...] + kn_rot * sinb_ref[...]).astype(BF16)
    vb_ref[0] = proj(OFF_VB, B_KV_COLS).astype(BF16)


def _prep(x, anw, w_fused, tables, qnw, qnwp, knw, knwp, ones, tm):
    b, s, d = x.shape
    cosa, sina, cosb, sinb = tables
    full = lambda shape: pl.BlockSpec(shape, lambda bi, i: (0,) * len(shape))
    tab = pl.BlockSpec((tm, LANES), lambda bi, i: (i, 0))
    out = lambda n: pl.BlockSpec((1, tm, n), lambda bi, i: (bi, i, 0))
    shp = lambda n: jax.ShapeDtypeStruct((b, s, n), BF16)
    return pl.pallas_call(
        _prep_kernel,
        grid=(b, s // tm),
        in_specs=[pl.BlockSpec((1, tm, d), lambda bi, i: (bi, i, 0)),
                  full((1, d)), full((d, FUSED_COLS)), tab, tab, tab, tab,
                  full((1, B_Q_COLS)), full((1, B_Q_COLS)), full((1, B_KV_COLS)), full((1, B_KV_COLS)),
                  full((B_Q_COLS, B_Q_COLS))],
        out_specs=[out(2 * A_COLS), out(A_COLS), out(A_COLS), out(2 * B_Q_COLS), out(B_KV_COLS), out(B_KV_COLS)],
        out_shape=[shp(2 * A_COLS), shp(A_COLS), shp(A_COLS), shp(2 * B_Q_COLS), shp(B_KV_COLS), shp(B_KV_COLS)],
        compiler_params=_params("parallel", "parallel"),
        name="prep",
    )(x, anw, w_fused, cosa, sina, cosb, sinb, qnw, qnwp, knw, knwp, ones)


def _flash(qs, k_ref, v_ref, m_sc, l_sc, acc_sc, tk):
    m_sc[...] = jnp.full(m_sc.shape, -jnp.inf, F32)
    l_sc[...] = jnp.zeros(l_sc.shape, F32)
    acc_sc[...] = jnp.zeros(acc_sc.shape, F32)

    def body(j, carry):
        off = pl.multiple_of(j * tk, tk)
        k = k_ref[0, pl.ds(off, tk), :]
        v = v_ref[0, pl.ds(off, tk), :]
        s = lax.dot_general(qs, k, (((1,), (1,)), ((), ())), preferred_element_type=F32)
        m_prev = m_sc[...]
        m_new = jnp.maximum(m_prev, jnp.max(s, axis=1, keepdims=True))
        alpha = jnp.exp(m_prev - m_new)
        p = jnp.exp(s - m_new)
        l_sc[...] = alpha * l_sc[...] + jnp.sum(p, axis=1, keepdims=True)
        acc_sc[...] = alpha * acc_sc[...] + jnp.dot(p.astype(BF16), v, preferred_element_type=F32)
        m_sc[...] = m_new
        return carry

    lax.fori_loop(0, k_ref.shape[1] // tk, body, 0)
    return acc_sc[...] / l_sc[...]


def _diff_attn_kernel(q_ref, k_ref, v_ref, lq1_ref, lk1_ref, lq2_ref, lk2_ref, sw_ref,
                      o_ref, m_sc, l_sc, acc_sc, *, tk):
    tq = q_ref.shape[1]
    q = q_ref[0]
    qs = jnp.concatenate([q[:, :LANES], q[:, LANES:]], axis=0)
    o = _flash(qs, k_ref, v_ref, m_sc, l_sc, acc_sc, tk)
    lam = (jnp.exp(jnp.sum(lq1_ref[...] * lk1_ref[...], axis=1, keepdims=True))
           - jnp.exp(jnp.sum(lq2_ref[...] * lk2_ref[...], axis=1, keepdims=True)) + LAMBDA_INIT)
    d = o[:tq] - lam * o[tq:]
    o_ref[0] = (_rms(d) * sw_ref[...] * (1.0 - LAMBDA_INIT)).astype(BF16)


def _diff_attn(qa, ka, va, lq1, lk1, lq2, lk2, sw, tq, tk):
    b, s, _ = ka.shape
    small = pl.BlockSpec((1, A_DIM), lambda bi, h, i: (0, 0))
    kv = pl.BlockSpec((1, s, LANES), lambda bi, h, i: (bi, 0, h))
    return pl.pallas_call(
        functools.partial(_diff_attn_kernel, tk=tk),
        grid=(b, A_HEADS, s // tq),
        in_specs=[pl.BlockSpec((1, tq, 2 * LANES), lambda bi, h, i: (bi, i, h)), kv, kv,
                  small, small, small, small, pl.BlockSpec((1, A_VDIM), lambda bi, h, i: (0, 0))],
        out_specs=pl.BlockSpec((1, tq, LANES), lambda bi, h, i: (bi, i, h)),
        out_shape=jax.ShapeDtypeStruct((b, s, A_COLS), BF16),
        scratch_shapes=[pltpu.VMEM((2 * tq, 1), F32), pltpu.VMEM((2 * tq, 1), F32),
                        pltpu.VMEM((2 * tq, LANES), F32)],
        compiler_params=_params("parallel", "parallel", "arbitrary"),
        name="diff_attn",
    )(qa, ka, va, lq1, lk1, lq2, lk2, sw)


def _gqa_kernel(q_ref, k_ref, v_ref, o_ref, m_sc, l_sc, acc_sc, *, tk):
    tq = q_ref.shape[1]
    q = q_ref[0]
    qs = jnp.concatenate([q[:, h * LANES:(h + 1) * LANES] for h in range(B_HEADS)], axis=0)
    o = _flash(qs, k_ref, v_ref, m_sc, l_sc, acc_sc, tk)
    low = lax.broadcasted_iota(jnp.int32, (tq, LANES), 1) < HALF
    chunks = []
    for c in range(B_HEADS // 2):
        even = o[(2 * c) * tq:(2 * c + 1) * tq]
        odd = o[(2 * c + 1) * tq:(2 * c + 2) * tq]
        if (2 * c) // B_GROUP == 0:
            chunks.append(jnp.where(low, even, pltpu.roll(odd, HALF, axis=1)))
        else:
            chunks.append(jnp.where(low, pltpu.roll(even, HALF, axis=1), odd))
    o_ref[0] = jnp.concatenate(chunks, axis=1).astype(BF16)


def _gqa_attn(qb, kb, vb, tq, tk):
    b, s, _ = kb.shape
    kv = pl.BlockSpec((1, s, LANES), lambda bi, i: (bi, 0, 0))
    return pl.pallas_call(
        functools.partial(_gqa_kernel, tk=tk),
        grid=(b, s // tq),
        in_specs=[pl.BlockSpec((1, tq, B_HEADS * LANES), lambda bi, i: (bi, i, 0)), kv, kv],
        out_specs=pl.BlockSpec((1, tq, B_Q_COLS), lambda bi, i: (bi, i, 0)),
        out_shape=jax.ShapeDtypeStruct((b, s, B_Q_COLS), BF16),
        scratch_shapes=[pltpu.VMEM((B_HEADS * tq, 1), F32), pltpu.VMEM((B_HEADS * tq, 1), F32),
                        pltpu.VMEM((B_HEADS * tq, LANES), F32)],
        compiler_params=_params("parallel", "arbitrary"),
        name="gqa_attn",
    )(qb, kb, vb)


def _mix_kernel(oa_ref, ob_ref, x_ref, wo_ref, fnw_ref, wq_ref, keys_ref, x1_ref, xn_ref, sc_ref):
    y = (jnp.dot(oa_ref[...], wo_ref[:A_COLS, :], preferred_element_type=F32)
         + jnp.dot(ob_ref[...], wo_ref[A_COLS:, :], preferred_element_type=F32))
    x1 = x_ref[...] + y
    xn = _rms(x1) * fnw_ref[...]
    x1_ref[...] = x1
    xn_ref[...] = xn
    xb = xn.astype(BF16)
    for hc in range(2 * PEER_HEADS):
        q = jnp.dot(xb, wq_ref[:, hc * PEER_D_HALF:(hc + 1) * PEER_D_HALF], preferred_element_type=F32)
        sc_ref[hc] = lax.dot_general(keys_ref[hc], q.astype(BF16), (((1,), (1,)), ((), ())),
                                     preferred_element_type=F32)


def _mix(oa, ob, x, wo, fnw, wq, keys, tm):
    t, d = x.shape
    nq = wq.shape[1]
    row = lambda n: pl.BlockSpec((tm, n), lambda i: (i, 0))
    full = lambda shape: pl.BlockSpec(shape, lambda i: (0,) * len(shape))
    return pl.pallas_call(
        _mix_kernel,
        grid=(t // tm,),
        in_specs=[row(A_COLS), row(B_Q_COLS), row(d), full((d, d)), full((1, d)), full((d, nq)),
                  full((2 * PEER_HEADS, PEER_N_KEYS, PEER_D_HALF))],
        out_specs=[row(d), row(d),
                   pl.BlockSpec((2 * PEER_HEADS, PEER_N_KEYS, tm), lambda i: (0, 0, i))],
        out_shape=[jax.ShapeDtypeStruct((t, d), F32), jax.ShapeDtypeStruct((t, d), F32),
                   jax.ShapeDtypeStruct((2 * PEER_HEADS, PEER_N_KEYS, t), F32)],
        compiler_params=_params("parallel"),
        name="mix",
    )(oa, ob, x, wo, fnw, wq, keys)


def _topk_rows(x, iota, k):
    n = float(x.shape[0])
    vals, rows = [], []
    for _ in range(k):
        m = jnp.max(x, axis=0, keepdims=True)
        r = jnp.min(jnp.where(x == m, iota, n), axis=0, keepdims=True)
        vals.append(m)
        rows.append(r)
        x = jnp.where(iota == r, -jnp.inf, x)
    return jnp.concatenate(vals, axis=0), jnp.concatenate(rows, axis=0)


def _select_rows(table, sel):
    out = jnp.zeros_like(sel)
    for r in range(table.shape[0]):
        out = jnp.where(sel == float(r), table[r:r + 1, :], out)
    return out


def _topk_kernel(sc_ref, idx_ref, g_ref, idx_sc):
    tt = sc_ref.shape[2]
    iota_keys = lax.broadcasted_iota(jnp.int32, (PEER_N_KEYS, tt), 0).astype(F32)
    iota_cand = lax.broadcasted_iota(jnp.int32, (PEER_TOPK * PEER_TOPK, tt), 0).astype(F32)

    def head(h, carry):
        s1, i1 = _topk_rows(sc_ref[2 * h], iota_keys, PEER_TOPK)
        s2, i2 = _topk_rows(sc_ref[2 * h + 1], iota_keys, PEER_TOPK)
        cand = jnp.concatenate([s1[a:a + 1, :] + s2 for a in range(PEER_TOPK)], axis=0)
        top_s, pos = _topk_rows(cand, iota_cand, PEER_TOPK)
        a = jnp.floor(pos * (1.0 / PEER_TOPK))
        b = pos - a * PEER_TOPK
        expert = _select_rows(i1, a) * PEER_N_KEYS + _select_rows(i2, b)
        e = jnp.exp(top_s - top_s[0:1, :])
        off = pl.multiple_of(h * PEER_TOPK, PEER_TOPK)
        idx_sc[pl.ds(off, PEER_TOPK), :] = expert
        g_ref[pl.ds(off, PEER_TOPK), :] = e / jnp.sum(e, axis=0, keepdims=True)
        return carry

    lax.fori_loop(0, PEER_HEADS, head, 0)
    idx_ref[...] = idx_sc[...].T.astype(jnp.int32)


def _topk(sc, tt):
    t = sc.shape[2]
    return pl.pallas_call(
        _topk_kernel,
        grid=(t // tt,),
        in_specs=[pl.BlockSpec((2 * PEER_HEADS, PEER_N_KEYS, tt), lambda i: (0, 0, i))],
        out_specs=[pl.BlockSpec((tt, PEER_PICKS), lambda i: (i, 0)),
                   pl.BlockSpec((PEER_PICKS, tt), lambda i: (0, i))],
        out_shape=[jax.ShapeDtypeStruct((t, PEER_PICKS), jnp.int32),
                   jax.ShapeDtypeStruct((PEER_PICKS, t), F32)],
        scratch_shapes=[pltpu.VMEM((PEER_PICKS, tt), F32)],
        compiler_params=_params("parallel"),
        name="topk",
    )(sc)


def _gelu(x):
    return 0.5 * x * (1.0 + lax.erf(x * (1.0 / math.sqrt(2.0))))


def _peer_kernel(idx_ref, g_ref, xn_ref, x1_ref, fw_ref, u_hbm, v_hbm, o_ref, ubuf, vbuf, sem):
    tb = xn_ref.shape[0]

    def row_copy(table, buf, which, slot, expert, j):
        return pltpu.make_async_copy(table.at[pl.ds(expert, 1)], buf.at[slot, pl.ds(j, 1)], sem.at[which, slot])

    def issue(t, slot):
        for j in range(PEER_PICKS):
            expert = idx_ref[t, j]
            row_copy(u_hbm, ubuf, 0, slot, expert, j).start()
            row_copy(v_hbm, vbuf, 1, slot, expert, j).start()

    def wait(slot):
        pltpu.make_async_copy(u_hbm.at[pl.ds(0, PEER_PICKS)], ubuf.at[slot], sem.at[0, slot]).wait()
        pltpu.make_async_copy(v_hbm.at[pl.ds(0, PEER_PICKS)], vbuf.at[slot], sem.at[1, slot]).wait()

    issue(0, 0)
    lane = lax.broadcasted_iota(jnp.int32, (PEER_PICKS, tb), 1)

    def body(t, carry):
        slot = t % 2

        @pl.when(t + 1 < tb)
        def _():
            issue(t + 1, 1 - slot)

        wait(slot)
        x = xn_ref[pl.ds(t, 1), :]
        act = jnp.sum(ubuf[slot] * x, axis=1, keepdims=True)
        gate = jnp.sum(jnp.where(lane == t, g_ref[...], 0.0), axis=1, keepdims=True)
        w = gate * _gelu(act)
        y = jnp.sum(w * vbuf[slot], axis=0, keepdims=True)
        r = x1_ref[pl.ds(t, 1), :] + y
        o_ref[pl.ds(t, 1), :] = _rms(r) * fw_ref[...]
        return carry

    lax.fori_loop(0, tb, body, 0)


def _peer(idx, g, xn, x1, fw, u, v, tb):
    t, d = xn.shape
    row = pl.BlockSpec((tb, d), lambda i: (i, 0))
    return pl.pallas_call(
        _peer_kernel,
        grid=(t // tb,),
        in_specs=[pl.BlockSpec((tb, PEER_PICKS), lambda i: (i, 0), memory_space=pltpu.SMEM),
                  pl.BlockSpec((PEER_PICKS, tb), lambda i: (0, i)),
                  row, row, pl.BlockSpec((1, d), lambda i: (0, 0)),
                  pl.BlockSpec(memory_space=pl.ANY), pl.BlockSpec(memory_space=pl.ANY)],
        out_specs=row,
        out_shape=jax.ShapeDtypeStruct((t, d), F32),
        scratch_shapes=[pltpu.VMEM((2, PEER_PICKS, d), F32), pltpu.VMEM((2, PEER_PICKS, d), F32),
                        pltpu.SemaphoreType.DMA((2, 2))],
        compiler_params=_params("arbitrary"),
        name="peer",
    )(idx, g, xn, x1, fw, u, v)


def _forward(x, p):
    b, s, d = x.shape
    tm = min(512, s)
    qa, ka, va, qb, kb, vb = _prep(x, p["anw"], p["w_fused"], _position_tables(s),
                                   p["qnw"], p["qnwp"], p["knw"], p["knwp"], p["ones"], tm)
    oa = _diff_attn(qa, ka, va, p["lq1"], p["lk1"], p["lq2"], p["lk2"], p["sw"], min(256, s), min(512, s))
    ob = _gqa_attn(qb, kb, vb, min(128, s), min(512, s))
    t = b * s
    x1, xn, sc = _mix(oa.reshape(t, A_COLS), ob.reshape(t, B_Q_COLS), x.reshape(t, d),
                      p["wo"], p["fnw"], p["wq"], p["keys"], min(256, t))
    idx, g = _topk(sc, LANES)
    y = _peer(idx, g, xn, x1, p["fw"], p["u"], p["v"], LANES)
    return y.reshape(b, s, d)


def kernel(x_prompt, x_sample, attn_norm, w_in, lambda_q1, lambda_k1, lambda_q2, lambda_k2, subln_w,
           q_norm_w, k_norm_w, w_out, ffn_norm, peer_wq, peer_keys, peer_u, peer_v, final_norm):
    perm_b, _ = _rotate_half_perm(B_Q_COLS, B_DIM // 2)
    qnw = jnp.tile(q_norm_w[0], B_HEADS)
    knw = jnp.tile(k_norm_w[0], B_KV_HEADS)
    p = dict(
        anw=attn_norm[0][None, :],
        w_fused=_fused_in_weight(w_in[0]),
        qnw=qnw[None, :], qnwp=qnw[perm_b][None, :],
        knw=knw[None, :], knwp=knw[perm_b[:B_KV_COLS]][None, :],
        ones=_group_ones(B_Q_COLS, B_DIM),
        lq1=lambda_q1[0][None, :], lk1=lambda_k1[0][None, :],
        lq2=lambda_q2[0][None, :], lk2=lambda_k2[0][None, :],
        sw=subln_w[0][None, :],
        wo=w_out[0].astype(BF16), fnw=ffn_norm[0][None, :],
        wq=peer_wq[0].astype(BF16),
        keys=peer_keys[0].reshape(2 * PEER_HEADS, PEER_N_KEYS, PEER_D_HALF).astype(BF16),
        u=peer_u[0], v=peer_v[0], fw=final_norm[None, :],
    )
    return _forward(x_prompt, p), _forward(x_sample, p)
```

```python
import functools
import math

import numpy as np
import jax
import jax.numpy as jnp
from jax import lax
from jax.experimental import pallas as pl
from jax.experimental.pallas import tpu as pltpu

F32 = jnp.float32
BF16 = jnp.bfloat16

D_MODEL = 1024
A_HEADS = 4
A_DIM = 64
A_VDIM = 2 * A_DIM
B_HEADS = 8
B_KV_HEADS = 2
B_DIM = 64
B_GROUP = B_HEADS // B_KV_HEADS
A_COLS = A_HEADS * 2 * A_DIM
B_Q_COLS = B_HEADS * B_DIM
B_KV_COLS = B_KV_HEADS * B_DIM
ROPE_THETA = 10000.0
GRID_W = 64
NORM_EPS = 1e-6
LAMBDA_INIT = 0.8 - 0.6 * math.exp(-0.3)
PEER_HEADS = 8
PEER_N_KEYS = 128
PEER_D_HALF = 128
PEER_TOPK = 16
PEER_PICKS = PEER_HEADS * PEER_TOPK

LANES = 128
HALF = LANES // 2
BF16_SUBLANES = 16
V_ROWS = LANES + BF16_SUBLANES
VMEM_LIMIT_BYTES = 56 * 1024 * 1024

OFF_QA, OFF_QA_ROT = 0, 512
OFF_KA, OFF_KA_ROT = 1024, 1536
OFF_VA = 2048
OFF_QB, OFF_QB_ROT = 2560, 3072
OFF_KB, OFF_KB_ROT = 3584, 3712
OFF_VB = 3840
FUSED_COLS = 3968


def _params(*semantics):
    return pltpu.CompilerParams(dimension_semantics=semantics, vmem_limit_bytes=VMEM_LIMIT_BYTES)


def _rope_tables(pos, dim):
    inv = ROPE_THETA ** (-jnp.arange(0, dim, 2, dtype=F32) / dim)
    ang = pos[:, None] * inv[None, :]
    ang = jnp.concatenate([ang, ang], axis=-1)
    return jnp.cos(ang), jnp.sin(ang)


def _position_tables(s):
    rows = s // GRID_W
    t = jnp.arange(s, dtype=F32)
    row = jnp.broadcast_to(jnp.arange(rows, dtype=F32)[:, None], (rows, GRID_W)).reshape(s)
    col = jnp.broadcast_to(jnp.arange(GRID_W, dtype=F32)[None, :], (rows, GRID_W)).reshape(s)
    cos_a, sin_a = _rope_tables(t, A_DIM)
    cos_r, sin_r = _rope_tables(row, B_DIM // 2)
    cos_c, sin_c = _rope_tables(col, B_DIM // 2)
    cos_b = jnp.concatenate([cos_r, cos_c], axis=-1)
    sin_b = jnp.concatenate([sin_r, sin_c], axis=-1)
    two = lambda a: jnp.concatenate([a, a], axis=-1)
    return two(cos_a), two(sin_a), two(cos_b), two(sin_b)


def _rotate_half_perm(n_cols, block):
    i = np.arange(n_cols)
    o = i % block
    half = block // 2
    perm = np.where(o < half, i + half, i - half)
    sign = np.where(o < half, -1.0, 1.0).astype(np.float32)
    return perm, sign


def _fused_in_weight(w_in):
    wqa = w_in[:, 0:512]
    wka = w_in[:, 512:1024]
    wva = w_in[:, 1024:1536]
    wqb = w_in[:, 1536:2048]
    wkb = w_in[:, 2048:2176]
    wvb = w_in[:, 2176:2304]

    def rot(m, block):
        perm, sign = _rotate_half_perm(m.shape[1], block)
        return m[:, perm] * sign

    parts = [wqa, rot(wqa, A_DIM), wka, rot(wka, A_DIM), wva,
             wqb, rot(wqb, B_DIM // 2), wkb, rot(wkb, B_DIM // 2), wvb]
    return jnp.concatenate(parts, axis=1).astype(BF16)


def _group_ones(n, group):
    i = np.arange(n) // group
    return jnp.asarray((i[:, None] == i[None, :]).astype(np.float32), dtype=BF16)


def _rms(x):
    return x * lax.rsqrt(jnp.mean(x * x, axis=-1, keepdims=True) + NORM_EPS)


def _group_mean_sq(p, ones):
    sq = p * p
    hi = sq.astype(BF16)
    lo = (sq - hi.astype(F32)).astype(BF16)
    ss = jnp.dot(hi, ones, preferred_element_type=F32) + jnp.dot(lo, ones, preferred_element_type=F32)
    return ss * (1.0 / B_DIM)


def _prep_kernel(x_ref, anw_ref, w_ref, cosa_ref, sina_ref, cosb_ref, sinb_ref,
                 qnw_ref, qnwp_ref, knw_ref, knwp_ref, ones_ref,
                 qat_ref, ka_ref, vat_ref, qbt_ref, kb_ref, vbt_ref):
    x = x_ref[0]
    tm = x.shape[0]
    hb = (_rms(x) * anw_ref[...]).astype(BF16)

    def proj(off, n):
        return jnp.dot(hb, w_ref[:, off:off + n], preferred_element_type=F32)

    def tile_lanes(a, n):
        return jnp.concatenate([a] * n, axis=1)

    zeros_half = jnp.zeros((HALF, tm), F32)
    ones_rows = jnp.ones((V_ROWS - LANES, tm), BF16)

    cosa = tile_lanes(cosa_ref[...], A_COLS // LANES)
    sina = tile_lanes(sina_ref[...], A_COLS // LANES)
    qa = (proj(OFF_QA, A_COLS) * cosa + proj(OFF_QA_ROT, A_COLS) * sina) * (A_DIM ** -0.5)
    ka = proj(OFF_KA, A_COLS) * cosa + proj(OFF_KA_ROT, A_COLS) * sina
    va = proj(OFF_VA, A_COLS)
    for h in range(A_HEADS):
        qt = qa[:, h * LANES:(h + 1) * LANES].T
        qat_ref[0, 2 * h] = jnp.concatenate([qt[:HALF], zeros_half], axis=0).astype(BF16)
        qat_ref[0, 2 * h + 1] = jnp.concatenate([zeros_half, qt[HALF:]], axis=0).astype(BF16)
        vat_ref[0, h, 0, :LANES, :] = va[:, h * LANES:(h + 1) * LANES].T.astype(BF16)
        vat_ref[0, h, 0, LANES:, :] = ones_rows
    ka_ref[0] = ka.astype(BF16)

    cosb = tile_lanes(cosb_ref[...], B_Q_COLS // LANES)
    sinb = tile_lanes(sinb_ref[...], B_Q_COLS // LANES)
    pq = proj(OFF_QB, B_Q_COLS)
    rq = lax.rsqrt(_group_mean_sq(pq, ones_ref[...]) + NORM_EPS)
    qn = pq * rq * qnw_ref[...]
    qn_rot = proj(OFF_QB_ROT, B_Q_COLS) * rq * qnwp_ref[...]
    qb = (qn * cosb + qn_rot * sinb) * (B_DIM ** -0.5)
    for c in range(B_HEADS // 2):
        qt = qb[:, c * LANES:(c + 1) * LANES].T
        for parity in range(2):
            h = 2 * c + parity
            piece = qt[parity * HALF:(parity + 1) * HALF]
            parts = [piece, zeros_half] if h // B_GROUP == 0 else [zeros_half, piece]
            qbt_ref[0, h] = jnp.concatenate(parts, axis=0).astype(BF16)

    pk = proj(OFF_KB, B_KV_COLS)
    rk = lax.rsqrt(_group_mean_sq(pk, ones_ref[:B_KV_COLS, :B_KV_COLS]) + NORM_EPS)
    kn = pk * rk * knw_ref[...]
    kn_rot = proj(OFF_KB_ROT, B_KV_COLS) * rk * knwp_ref[...]
    kb_ref[0] = (kn * cosb_ref[...] + kn_rot * sinb_ref[...]).astype(BF16)
    vbt_ref[0, 0, :LANES, :] = proj(OFF_VB, B_KV_COLS).T.astype(BF16)
    vbt_ref[0, 0, LANES:, :] = ones_rows


def _prep(x, anw, w_fused, tables, qnw, qnwp, knw, knwp, ones, tm):
    b, s, d = x.shape
    nt = s // tm
    cosa, sina, cosb, sinb = tables
    full = lambda shape: pl.BlockSpec(shape, lambda bi, i: (0,) * len(shape))
    tab = pl.BlockSpec((tm, LANES), lambda bi, i: (i, 0))
    tok = lambda n: pl.BlockSpec((1, tm, n), lambda bi, i: (bi, i, 0))
    qt = lambda heads: pl.BlockSpec((1, heads, LANES, tm), lambda bi, i: (bi, 0, 0, i))
    sds = lambda *shape: jax.ShapeDtypeStruct(shape, BF16)
    return pl.pallas_call(
        _prep_kernel,
        grid=(b, nt),
        in_specs=[pl.BlockSpec((1, tm, d), lambda bi, i: (bi, i, 0)),
                  full((1, d)), full((d, FUSED_COLS)), tab, tab, tab, tab,
                  full((1, B_Q_COLS)), full((1, B_Q_COLS)), full((1, B_KV_COLS)), full((1, B_KV_COLS)),
                  full((B_Q_COLS, B_Q_COLS))],
        out_specs=[qt(2 * A_HEADS), tok(A_COLS),
                   pl.BlockSpec((1, A_HEADS, 1, V_ROWS, tm), lambda bi, i: (bi, 0, i, 0, 0)),
                   qt(B_HEADS), tok(B_KV_COLS),
                   pl.BlockSpec((1, 1, V_ROWS, tm), lambda bi, i: (bi, i, 0, 0))],
        out_shape=[sds(b, 2 * A_HEADS, LANES, s), sds(b, s, A_COLS), sds(b, A_HEADS, nt, V_ROWS, tm),
                   sds(b, B_HEADS, LANES, s), sds(b, s, B_KV_COLS), sds(b, nt, V_ROWS, tm)],
        compiler_params=_params("parallel", "parallel"),
        name="prep",
    )(x, anw, w_fused, cosa, sina, cosb, sinb, qnw, qnwp, knw, knwp, ones)


def _flash_scratch(rows, tk):
    return [pltpu.VMEM((1, rows), F32), pltpu.VMEM((V_ROWS, rows), F32),
            pltpu.VMEM((tk, rows), F32), pltpu.VMEM((tk, rows), F32)]


def _flash_t(q_chunk, n_chunks, ch, k_ref, vt_tile, m_sc, acc_sc, s0_sc, s1_sc):
    m_sc[...] = jnp.full(m_sc.shape, -jnp.inf, F32)
    acc_sc[...] = jnp.zeros(acc_sc.shape, F32)
    tk = s0_sc.shape[0]
    nk = k_ref.shape[1] // tk
    assert nk >= 2 and nk % 2 == 0, "key tiles are processed in pairs"
    chunks = [slice(c * ch, (c + 1) * ch) for c in range(n_chunks)]

    def scores(j, c, dst):
        k = k_ref[0, pl.ds(pl.multiple_of(j * tk, tk), tk), :]
        dst[:, chunks[c]] = jnp.dot(k, q_chunk(c), preferred_element_type=F32)

    def softmax_pv(j, c, src):
        sl = chunks[c]
        s = src[:, sl]
        m_prev = m_sc[:, sl]
        m_new = jnp.maximum(m_prev, jnp.max(s, axis=0, keepdims=True))
        p = jnp.exp(s - m_new).astype(BF16)
        acc_sc[:, sl] = (jnp.exp(m_prev - m_new) * acc_sc[:, sl]
                         + jnp.dot(vt_tile(j), p, preferred_element_type=F32))
        m_sc[:, sl] = m_new

    def step(j, src, dst):
        for c in range(n_chunks):
            if dst is not None:
                scores(j + 1, c, dst)
            softmax_pv(j, c, src)

    for c in range(n_chunks):
        scores(0, c, s0_sc)

    def body(i, carry):
        step(2 * i, s0_sc, s1_sc)
        step(2 * i + 1, s1_sc, s0_sc)
        return carry

    lax.fori_loop(0, nk // 2 - 1, body, 0)
    step(nk - 2, s0_sc, s1_sc)
    step(nk - 1, s1_sc, None)
    acc = acc_sc[...]
    return acc[:LANES] / acc[LANES:LANES + 1]


def _diff_attn_kernel(qt_ref, k_ref, vt_ref, lq1_ref, lk1_ref, lq2_ref, lk2_ref, sw_ref,
                      o_ref, *scratch, ch):
    tq = qt_ref.shape[3]
    per_map = tq // ch

    def q_chunk(c):
        return qt_ref[0, c // per_map, :, (c % per_map) * ch:(c % per_map + 1) * ch]

    ot = _flash_t(q_chunk, 2 * per_map, ch, k_ref, lambda j: vt_ref[0, 0, j], *scratch)
    lam = (jnp.exp(jnp.sum(lq1_ref[...] * lk1_ref[...], axis=1, keepdims=True))
           - jnp.exp(jnp.sum(lq2_ref[...] * lk2_ref[...], axis=1, keepdims=True)) + LAMBDA_INIT)
    d = (ot[:, :tq] - lam * ot[:, tq:]).T
    o_ref[0] = (_rms(d) * sw_ref[...] * (1.0 - LAMBDA_INIT)).astype(BF16)


def _diff_attn(qat, ka, vat, lq1, lk1, lq2, lk2, sw, tq, tk, ch):
    b, s, _ = ka.shape
    small = pl.BlockSpec((1, A_DIM), lambda bi, h, i: (0, 0))
    return pl.pallas_call(
        functools.partial(_diff_attn_kernel, ch=ch),
        grid=(b, A_HEADS, s // tq),
        in_specs=[pl.BlockSpec((1, 2, LANES, tq), lambda bi, h, i: (bi, h, 0, i)),
                  pl.BlockSpec((1, s, LANES), lambda bi, h, i: (bi, 0, h)),
                  pl.BlockSpec((1, 1, s // tk, V_ROWS, tk), lambda bi, h, i: (bi, h, 0, 0, 0)),
                  small, small, small, small, pl.BlockSpec((1, A_VDIM), lambda bi, h, i: (0, 0))],
        out_specs=pl.BlockSpec((1, tq, LANES), lambda bi, h, i: (bi, i, h)),
        out_shape=jax.ShapeDtypeStruct((b, s, A_COLS), BF16),
        scratch_shapes=_flash_scratch(2 * tq, tk),
        compiler_params=_params("parallel", "parallel", "arbitrary"),
        name="diff_attn",
    )(qat, ka, vat, lq1, lk1, lq2, lk2, sw)


def _gqa_kernel(qt_ref, k_ref, vt_ref, o_ref, *scratch, ch):
    tq = qt_ref.shape[3]
    heads_per_chunk = ch // tq

    def q_chunk(c):
        heads = range(c * heads_per_chunk, (c + 1) * heads_per_chunk)
        return jnp.concatenate([qt_ref[0, h] for h in heads], axis=1)

    ot = _flash_t(q_chunk, B_HEADS // heads_per_chunk, ch, k_ref, lambda j: vt_ref[0, j], *scratch)
    chunks = []
    for c in range(B_HEADS // 2):
        rows = slice(((2 * c) // B_GROUP) * HALF, ((2 * c) // B_GROUP + 1) * HALF)
        pair = jnp.concatenate([ot[rows, (2 * c) * tq:(2 * c + 1) * tq],
                                ot[rows, (2 * c + 1) * tq:(2 * c + 2) * tq]], axis=0)
        chunks.append(pair.T)
    o_ref[0] = jnp.concatenate(chunks, axis=1).astype(BF16)


def _gqa_attn(qbt, kb, vbt, tq, tk, ch):
    b, s, _ = kb.shape
    return pl.pallas_call(
        functools.partial(_gqa_kernel, ch=ch),
        grid=(b, s // tq),
        in_specs=[pl.BlockSpec((1, B_HEADS, LANES, tq), lambda bi, i: (bi, 0, 0, i)),
                  pl.BlockSpec((1, s, LANES), lambda bi, i: (bi, 0, 0)),
                  pl.BlockSpec((1, s // tk, V_ROWS, tk), lambda bi, i: (bi, 0, 0, 0))],
        out_specs=pl.BlockSpec((1, tq, B_Q_COLS), lambda bi, i: (bi, i, 0)),
        out_shape=jax.ShapeDtypeStruct((b, s, B_Q_COLS), BF16),
        scratch_shapes=_flash_scratch(B_HEADS * tq, tk),
        compiler_params=_params("parallel", "arbitrary"),
        name="gqa_attn",
    )(qbt, kb, vbt)


def _mix_kernel(oa_ref, ob_ref, x_ref, wo_ref, fnw_ref, wq_ref, keys_ref, x1_ref, xn_ref, sc_ref):
    y = (jnp.dot(oa_ref[...], wo_ref[:A_COLS, :], preferred_element_type=F32)
         + jnp.dot(ob_ref[...], wo_ref[A_COLS:, :], preferred_element_type=F32))
    x1 = x_ref[...] + y
    xn = _rms(x1) * fnw_ref[...]
    x1_ref[...] = x1
    xn_ref[...] = xn
    xb = xn.astype(BF16)
    for hc in range(2 * PEER_HEADS):
        q = jnp.dot(xb, wq_ref[:, hc * PEER_D_HALF:(hc + 1) * PEER_D_HALF], preferred_element_type=F32)
        sc_ref[hc] = lax.dot_general(keys_ref[hc], q.astype(BF16), (((1,), (1,)), ((), ())),
                                     preferred_element_type=F32)


def _mix(oa, ob, x, wo, fnw, wq, keys, tm):
    t, d = x.shape
    nq = wq.shape[1]
    row = lambda n: pl.BlockSpec((tm, n), lambda i: (i, 0))
    full = lambda shape: pl.BlockSpec(shape, lambda i: (0,) * len(shape))
    return pl.pallas_call(
        _mix_kernel,
        grid=(t // tm,),
        in_specs=[row(A_COLS), row(B_Q_COLS), row(d), full((d, d)), full((1, d)), full((d, nq)),
                  full((2 * PEER_HEADS, PEER_N_KEYS, PEER_D_HALF))],
        out_specs=[row(d), row(d),
                   pl.BlockSpec((2 * PEER_HEADS, PEER_N_KEYS, tm), lambda i: (0, 0, i))],
        out_shape=[jax.ShapeDtypeStruct((t, d), F32), jax.ShapeDtypeStruct((t, d), F32),
                   jax.ShapeDtypeStruct((2 * PEER_HEADS, PEER_N_KEYS, t), F32)],
        compiler_params=_params("parallel"),
        name="mix",
    )(oa, ob, x, wo, fnw, wq, keys)


def _topk_rows(x, iota, k):
    n = float(x.shape[0])
    vals, rows = [], []
    for _ in range(k):
        m = jnp.max(x, axis=0, keepdims=True)
        r = jnp.min(jnp.where(x == m, iota, n), axis=0, keepdims=True)
        vals.append(m)
        rows.append(r)
        x = jnp.where(iota == r, -jnp.inf, x)
    return jnp.concatenate(vals, axis=0), jnp.concatenate(rows, axis=0)


def _select_rows(table, sel):
    out = jnp.zeros_like(sel)
    for r in range(table.shape[0]):
        out = jnp.where(sel == float(r), table[r:r + 1, :], out)
    return out


def _topk_kernel(sc_ref, idx_ref, g_ref, idx_sc):
    tt = sc_ref.shape[2]
    iota_keys = lax.broadcasted_iota(jnp.int32, (PEER_N_KEYS, tt), 0).astype(F32)
    iota_cand = lax.broadcasted_iota(jnp.int32, (PEER_TOPK * PEER_TOPK, tt), 0).astype(F32)

    def head(h, carry):
        s1, i1 = _topk_rows(sc_ref[2 * h], iota_keys, PEER_TOPK)
        s2, i2 = _topk_rows(sc_ref[2 * h + 1], iota_keys, PEER_TOPK)
        cand = jnp.concatenate([s1[a:a + 1, :] + s2 for a in range(PEER_TOPK)], axis=0)
        top_s, pos = _topk_rows(cand, iota_cand, PEER_TOPK)
        a = jnp.floor(pos * (1.0 / PEER_TOPK))
        b = pos - a * PEER_TOPK
        expert = _select_rows(i1, a) * PEER_N_KEYS + _select_rows(i2, b)
        e = jnp.exp(top_s - top_s[0:1, :])
        off = pl.multiple_of(h * PEER_TOPK, PEER_TOPK)
        idx_sc[pl.ds(off, PEER_TOPK), :] = expert
        g_ref[pl.ds(off, PEER_TOPK), :] = e / jnp.sum(e, axis=0, keepdims=True)
        return carry

    lax.fori_loop(0, PEER_HEADS, head, 0)
    idx_ref[...] = idx_sc[...].T.astype(jnp.int32)


def _topk(sc, tt):
    t = sc.shape[2]
    return pl.pallas_call(
        _topk_kernel,
        grid=(t // tt,),
        in_specs=[pl.BlockSpec((2 * PEER_HEADS, PEER_N_KEYS, tt), lambda i: (0, 0, i))],
        out_specs=[pl.BlockSpec((tt, PEER_PICKS), lambda i: (i, 0)),
                   pl.BlockSpec((PEER_PICKS, tt), lambda i: (0, i))],
        out_shape=[jax.ShapeDtypeStruct((t, PEER_PICKS), jnp.int32),
                   jax.ShapeDtypeStruct((PEER_PICKS, t), F32)],
        scratch_shapes=[pltpu.VMEM((PEER_PICKS, tt), F32)],
        compiler_params=_params("parallel"),
        name="topk",
    )(sc)


def _gelu(x):
    return 0.5 * x * (1.0 + lax.erf(x * (1.0 / math.sqrt(2.0))))


def _peer_kernel(idx_ref, g_ref, xn_ref, x1_ref, fw_ref, u_hbm, v_hbm, o_ref, ubuf, vbuf, sem):
    tb = xn_ref.shape[0]

    def row_copy(table, buf, which, slot, expert, j):
        return pltpu.make_async_copy(table.at[pl.ds(expert, 1)], buf.at[slot, pl.ds(j, 1)], sem.at[which, slot])

    def issue(t, slot):
        for j in range(PEER_PICKS):
            expert = idx_ref[t, j]
            row_copy(u_hbm, ubuf, 0, slot, expert, j).start()
            row_copy(v_hbm, vbuf, 1, slot, expert, j).start()

    def wait(slot):
        pltpu.make_async_copy(u_hbm.at[pl.ds(0, PEER_PICKS)], ubuf.at[slot], sem.at[0, slot]).wait()
        pltpu.make_async_copy(v_hbm.at[pl.ds(0, PEER_PICKS)], vbuf.at[slot], sem.at[1, slot]).wait()

    issue(0, 0)
    lane = lax.broadcasted_iota(jnp.int32, (PEER_PICKS, tb), 1)

    def body(t, carry):
        slot = t % 2

        @pl.when(t + 1 < tb)
        def _():
            issue(t + 1, 1 - slot)

        wait(slot)
        x = xn_ref[pl.ds(t, 1), :]
        act = jnp.sum(ubuf[slot] * x, axis=1, keepdims=True)
        gate = jnp.sum(jnp.where(lane == t, g_ref[...], 0.0), axis=1, keepdims=True)
        w = gate * _gelu(act)
        y = jnp.sum(w * vbuf[slot], axis=0, keepdims=True)
        r = x1_ref[pl.ds(t, 1), :] + y
        o_ref[pl.ds(t, 1), :] = _rms(r) * fw_ref[...]
        return carry

    lax.fori_loop(0, tb, body, 0)


def _peer(idx, g, xn, x1, fw, u, v, tb):
    t, d = xn.shape
    row = pl.BlockSpec((tb, d), lambda i: (i, 0))
    return pl.pallas_call(
        _peer_kernel,
        grid=(t // tb,),
        in_specs=[pl.BlockSpec((tb, PEER_PICKS), lambda i: (i, 0), memory_space=pltpu.SMEM),
                  pl.BlockSpec((PEER_PICKS, tb), lambda i: (0, i)),
                  row, row, pl.BlockSpec((1, d), lambda i: (0, 0)),
                  pl.BlockSpec(memory_space=pl.ANY), pl.BlockSpec(memory_space=pl.ANY)],
        out_specs=row,
        out_shape=jax.ShapeDtypeStruct((t, d), F32),
        scratch_shapes=[pltpu.VMEM((2, PEER_PICKS, d), F32), pltpu.VMEM((2, PEER_PICKS, d), F32),
                        pltpu.SemaphoreType.DMA((2, 2))],
        compiler_params=_params("arbitrary"),
        name="peer",
    )(idx, g, xn, x1, fw, u, v)


def _forward(x, p):
    b, s, d = x.shape
    tk = min(512, s // 2)
    ch = 2 * LANES
    qat, ka, vat, qbt, kb, vbt = _prep(x, p["anw"], p["w_fused"], _position_tables(s),
                                       p["qnw"], p["qnwp"], p["knw"], p["knwp"], p["ones"], tk)
    oa = _diff_attn(qat, ka, vat, p["lq1"], p["lk1"], p["lq2"], p["lk2"], p["sw"], min(512, s), tk, ch)
    ob = _gqa_attn(qbt, kb, vbt, min(LANES, s), tk, ch)
    t = b * s
    x1, xn, sc = _mix(oa.reshape(t, A_COLS), ob.reshape(t, B_Q_COLS), x.reshape(t, d),
                      p["wo"], p["fnw"], p["wq"], p["keys"], min(256, t))
    idx, g = _topk(sc, LANES)
    y = _peer(idx, g, xn, x1, p["fw"], p["u"], p["v"], LANES)
    return y.reshape(b, s, d)


def kernel(x_prompt, x_sample, attn_norm, w_in, lambda_q1, lambda_k1, lambda_q2, lambda_k2, subln_w,
           q_norm_w, k_norm_w, w_out, ffn_norm, peer_wq, peer_keys, peer_u, peer_v, final_norm):
    perm_b, _ = _rotate_half_perm(B_Q_COLS, B_DIM // 2)
    qnw = jnp.tile(q_norm_w[0], B_HEADS)
    knw = jnp.tile(k_norm_w[0], B_KV_HEADS)
    p = dict(
        anw=attn_norm[0][None, :],
        w_fused=_fused_in_weight(w_in[0]),
        qnw=qnw[None, :], qnwp=qnw[perm_b][None, :],
        knw=knw[None, :], knwp=knw[perm_b[:B_KV_COLS]][None, :],
        ones=_group_ones(B_Q_COLS, B_DIM),
        lq1=lambda_q1[0][None, :], lk1=lambda_k1[0][None, :],
        lq2=lambda_q2[0][None, :], lk2=lambda_k2[0][None, :],
        sw=subln_w[0][None, :],
        wo=w_out[0].astype(BF16), fnw=ffn_norm[0][None, :],
        wq=peer_wq[0].astype(BF16),
        keys=peer_keys[0].reshape(2 * PEER_HEADS, PEER_N_KEYS, PEER_D_HALF).astype(BF16),
        u=peer_u[0], v=peer_v[0], fw=final_norm[None, :],
    )
    return _forward(x_prompt, p), _forward(x_sample, p)
```

```python
import functools
import math

import numpy as np
import jax
import jax.numpy as jnp
from jax import lax
from jax.experimental import pallas as pl
from jax.experimental.pallas import tpu as pltpu

F32 = jnp.float32
BF16 = jnp.bfloat16

D_MODEL = 1024
A_HEADS = 4
A_DIM = 64
A_VDIM = 2 * A_DIM
B_HEADS = 8
B_KV_HEADS = 2
B_DIM = 64
B_GROUP = B_HEADS // B_KV_HEADS
A_COLS = A_HEADS * 2 * A_DIM
B_Q_COLS = B_HEADS * B_DIM
B_KV_COLS = B_KV_HEADS * B_DIM
ROPE_THETA = 10000.0
GRID_W = 64
NORM_EPS = 1e-6
LAMBDA_INIT = 0.8 - 0.6 * math.exp(-0.3)
PEER_HEADS = 8
PEER_N_KEYS = 128
PEER_D_HALF = 128
PEER_TOPK = 16
PEER_PICKS = PEER_HEADS * PEER_TOPK

LANES = 128
HALF = LANES // 2
BF16_SUBLANES = 16
V_ROWS = LANES + BF16_SUBLANES
VMEM_LIMIT_BYTES = 56 * 1024 * 1024

OFF_QA, OFF_QA_ROT = 0, 512
OFF_KA, OFF_KA_ROT = 1024, 1536
OFF_VA = 2048
OFF_QB, OFF_QB_ROT = 2560, 3072
OFF_KB, OFF_KB_ROT = 3584, 3712
OFF_VB = 3840
FUSED_COLS = 3968


def _params(*semantics):
    return pltpu.CompilerParams(dimension_semantics=semantics, vmem_limit_bytes=VMEM_LIMIT_BYTES)


def _rope_tables(pos, dim):
    inv = ROPE_THETA ** (-jnp.arange(0, dim, 2, dtype=F32) / dim)
    ang = pos[:, None] * inv[None, :]
    ang = jnp.concatenate([ang, ang], axis=-1)
    return jnp.cos(ang), jnp.sin(ang)


def _position_tables(s):
    rows = s // GRID_W
    t = jnp.arange(s, dtype=F32)
    row = jnp.broadcast_to(jnp.arange(rows, dtype=F32)[:, None], (rows, GRID_W)).reshape(s)
    col = jnp.broadcast_to(jnp.arange(GRID_W, dtype=F32)[None, :], (rows, GRID_W)).reshape(s)
    cos_a, sin_a = _rope_tables(t, A_DIM)
    cos_r, sin_r = _rope_tables(row, B_DIM // 2)
    cos_c, sin_c = _rope_tables(col, B_DIM // 2)
    cos_b = jnp.concatenate([cos_r, cos_c], axis=-1)
    sin_b = jnp.concatenate([sin_r, sin_c], axis=-1)
    two = lambda a: jnp.concatenate([a, a], axis=-1)
    return two(cos_a), two(sin_a), two(cos_b), two(sin_b)


def _rotate_half_perm(n_cols, block):
    i = np.arange(n_cols)
    o = i % block
    half = block // 2
    perm = np.where(o < half, i + half, i - half)
    sign = np.where(o < half, -1.0, 1.0).astype(np.float32)
    return perm, sign


def _fused_in_weight(w_in):
    wqa = w_in[:, 0:512]
    wka = w_in[:, 512:1024]
    wva = w_in[:, 1024:1536]
    wqb = w_in[:, 1536:2048]
    wkb = w_in[:, 2048:2176]
    wvb = w_in[:, 2176:2304]

    def rot(m, block):
        perm, sign = _rotate_half_perm(m.shape[1], block)
        return m[:, perm] * sign

    parts = [wqa, rot(wqa, A_DIM), wka, rot(wka, A_DIM), wva,
             wqb, rot(wqb, B_DIM // 2), wkb, rot(wkb, B_DIM // 2), wvb]
    return jnp.concatenate(parts, axis=1).astype(BF16)


def _group_ones(n, group):
    i = np.arange(n) // group
    return jnp.asarray((i[:, None] == i[None, :]).astype(np.float32), dtype=BF16)


def _rms(x):
    return x * lax.rsqrt(jnp.mean(x * x, axis=-1, keepdims=True) + NORM_EPS)


def _group_mean_sq(p, ones):
    sq = p * p
    hi = sq.astype(BF16)
    lo = (sq - hi.astype(F32)).astype(BF16)
    ss = jnp.dot(hi, ones, preferred_element_type=F32) + jnp.dot(lo, ones, preferred_element_type=F32)
    return ss * (1.0 / B_DIM)


def _prep_kernel(x_ref, anw_ref, w_ref, cosa_ref, sina_ref, cosb_ref, sinb_ref,
                 qnw_ref, qnwp_ref, knw_ref, knwp_ref, ones_ref,
                 qat_ref, ka_ref, vat_ref, qbt_ref, kb_ref, vbt_ref):
    x = x_ref[0]
    tm = x.shape[0]
    hb = (_rms(x) * anw_ref[...]).astype(BF16)

    def proj(off, n):
        return jnp.dot(hb, w_ref[:, off:off + n], preferred_element_type=F32)

    def tile_lanes(a, n):
        return jnp.concatenate([a] * n, axis=1)

    zeros_half = jnp.zeros((HALF, tm), F32)
    ones_rows = jnp.ones((V_ROWS - LANES, tm), BF16)

    cosa = tile_lanes(cosa_ref[...], A_COLS // LANES)
    sina = tile_lanes(sina_ref[...], A_COLS // LANES)
    qa = (proj(OFF_QA, A_COLS) * cosa + proj(OFF_QA_ROT, A_COLS) * sina) * (A_DIM ** -0.5)
    ka = proj(OFF_KA, A_COLS) * cosa + proj(OFF_KA_ROT, A_COLS) * sina
    va = proj(OFF_VA, A_COLS)
    for h in range(A_HEADS):
        qt = qa[:, h * LANES:(h + 1) * LANES].T
        qat_ref[0, 2 * h] = jnp.concatenate([qt[:HALF], zeros_half], axis=0).astype(BF16)
        qat_ref[0, 2 * h + 1] = jnp.concatenate([zeros_half, qt[HALF:]], axis=0).astype(BF16)
        vat_ref[0, h, 0, :LANES, :] = va[:, h * LANES:(h + 1) * LANES].T.astype(BF16)
        vat_ref[0, h, 0, LANES:, :] = ones_rows
    ka_ref[0] = ka.astype(BF16)

    cosb = tile_lanes(cosb_ref[...], B_Q_COLS // LANES)
    sinb = tile_lanes(sinb_ref[...], B_Q_COLS // LANES)
    pq = proj(OFF_QB, B_Q_COLS)
    rq = lax.rsqrt(_group_mean_sq(pq, ones_ref[...]) + NORM_EPS)
    qn = pq * rq * qnw_ref[...]
    qn_rot = proj(OFF_QB_ROT, B_Q_COLS) * rq * qnwp_ref[...]
    qb = (qn * cosb + qn_rot * sinb) * (B_DIM ** -0.5)
    for c in range(B_HEADS // 2):
        qt = qb[:, c * LANES:(c + 1) * LANES].T
        for parity in range(2):
            h = 2 * c + parity
            piece = qt[parity * HALF:(parity + 1) * HALF]
            parts = [piece, zeros_half] if h // B_GROUP == 0 else [zeros_half, piece]
            qbt_ref[0, h] = jnp.concatenate(parts, axis=0).astype(BF16)

    pk = proj(OFF_KB, B_KV_COLS)
    rk = lax.rsqrt(_group_mean_sq(pk, ones_ref[:B_KV_COLS, :B_KV_COLS]) + NORM_EPS)
    kn = pk * rk * knw_ref[...]
    kn_rot = proj(OFF_KB_ROT, B_KV_COLS) * rk * knwp_ref[...]
    kb_ref[0] = (kn * cosb_ref[...] + kn_rot * sinb_ref[...]).astype(BF16)
    vbt_ref[0, 0, :LANES, :] = proj(OFF_VB, B_KV_COLS).T.astype(BF16)
    vbt_ref[0, 0, LANES:, :] = ones_rows


def _prep(x, anw, w_fused, tables, qnw, qnwp, knw, knwp, ones, tm):
    b, s, d = x.shape
    nt = s // tm
    cosa, sina, cosb, sinb = tables
    full = lambda shape: pl.BlockSpec(shape, lambda bi, i: (0,) * len(shape))
    tab = pl.BlockSpec((tm, LANES), lambda bi, i: (i, 0))
    tok = lambda n: pl.BlockSpec((1, tm, n), lambda bi, i: (bi, i, 0))
    qt = lambda heads: pl.BlockSpec((1, heads, LANES, tm), lambda bi, i: (bi, 0, 0, i))
    sds = lambda *shape: jax.ShapeDtypeStruct(shape, BF16)
    return pl.pallas_call(
        _prep_kernel,
        grid=(b, nt),
        in_specs=[pl.BlockSpec((1, tm, d), lambda bi, i: (bi, i, 0)),
                  full((1, d)), full((d, FUSED_COLS)), tab, tab, tab, tab,
                  full((1, B_Q_COLS)), full((1, B_Q_COLS)), full((1, B_KV_COLS)), full((1, B_KV_COLS)),
                  full((B_Q_COLS, B_Q_COLS))],
        out_specs=[qt(2 * A_HEADS), tok(A_COLS),
                   pl.BlockSpec((1, A_HEADS, 1, V_ROWS, tm), lambda bi, i: (bi, 0, i, 0, 0)),
                   qt(B_HEADS), tok(B_KV_COLS),
                   pl.BlockSpec((1, 1, V_ROWS, tm), lambda bi, i: (bi, i, 0, 0))],
        out_shape=[sds(b, 2 * A_HEADS, LANES, s), sds(b, s, A_COLS), sds(b, A_HEADS, nt, V_ROWS, tm),
                   sds(b, B_HEADS, LANES, s), sds(b, s, B_KV_COLS), sds(b, nt, V_ROWS, tm)],
        compiler_params=_params("parallel", "parallel"),
        name="prep",
    )(x, anw, w_fused, cosa, sina, cosb, sinb, qnw, qnwp, knw, knwp, ones)


def _flash_scratch(rows, tk):
    return [pltpu.VMEM((1, rows), F32), pltpu.VMEM((V_ROWS, rows), F32),
            pltpu.VMEM((tk, rows), F32), pltpu.VMEM((tk, rows), F32)]


def _flash_t(q_chunk, n_chunks, ch, k_ref, vt_tile, m_sc, acc_sc, s0_sc, s1_sc):
    m_sc[...] = jnp.full(m_sc.shape, -jnp.inf, F32)
    acc_sc[...] = jnp.zeros(acc_sc.shape, F32)
    tk = s0_sc.shape[0]
    nk = k_ref.shape[1] // tk
    assert nk >= 2 and nk % 2 == 0, "key tiles are processed in pairs"
    chunks = [slice(c * ch, (c + 1) * ch) for c in range(n_chunks)]

    def scores(j, c, dst):
        k = k_ref[0, pl.ds(pl.multiple_of(j * tk, tk), tk), :]
        dst[:, chunks[c]] = jnp.dot(k, q_chunk(c), preferred_element_type=F32)

    def softmax_pv(j, c, src):
        sl = chunks[c]
        s = src[:, sl]
        m_prev = m_sc[:, sl]
        m_new = jnp.maximum(m_prev, jnp.max(s, axis=0, keepdims=True))
        p = jnp.exp(s - m_new).astype(BF16)
        acc_sc[:, sl] = (jnp.exp(m_prev - m_new) * acc_sc[:, sl]
                         + jnp.dot(vt_tile(j), p, preferred_element_type=F32))
        m_sc[:, sl] = m_new

    def step(j, src, dst):
        for c in range(n_chunks):
            if dst is not None:
                scores(j + 1, c, dst)
            softmax_pv(j, c, src)

    for c in range(n_chunks):
        scores(0, c, s0_sc)

    def body(i, carry):
        step(2 * i, s0_sc, s1_sc)
        step(2 * i + 1, s1_sc, s0_sc)
        return carry

    lax.fori_loop(0, nk // 2 - 1, body, 0)
    step(nk - 2, s0_sc, s1_sc)
    step(nk - 1, s1_sc, None)
    acc = acc_sc[...]
    return acc[:LANES] / acc[LANES:LANES + 1]


def _diff_attn_kernel(qt_ref, k_ref, vt_ref, lq1_ref, lk1_ref, lq2_ref, lk2_ref, sw_ref,
                      o_ref, *scratch, ch):
    tq = qt_ref.shape[3]
    per_map = tq // ch

    def q_chunk(c):
        return qt_ref[0, c // per_map, :, (c % per_map) * ch:(c % per_map + 1) * ch]

    ot = _flash_t(q_chunk, 2 * per_map, ch, k_ref, lambda j: vt_ref[0, 0, j], *scratch)
    lam = (jnp.exp(jnp.sum(lq1_ref[...] * lk1_ref[...], axis=1, keepdims=True))
           - jnp.exp(jnp.sum(lq2_ref[...] * lk2_ref[...], axis=1, keepdims=True)) + LAMBDA_INIT)
    d = (ot[:, :tq] - lam * ot[:, tq:]).T
    o_ref[0] = (_rms(d) * sw_ref[...] * (1.0 - LAMBDA_INIT)).astype(BF16)


def _diff_attn(qat, ka, vat, lq1, lk1, lq2, lk2, sw, tq, tk, ch):
    b, s, _ = ka.shape
    small = pl.BlockSpec((1, A_DIM), lambda bi, h, i: (0, 0))
    return pl.pallas_call(
        functools.partial(_diff_attn_kernel, ch=ch),
        grid=(b, A_HEADS, s // tq),
        in_specs=[pl.BlockSpec((1, 2, LANES, tq), lambda bi, h, i: (bi, h, 0, i)),
                  pl.BlockSpec((1, s, LANES), lambda bi, h, i: (bi, 0, h)),
                  pl.BlockSpec((1, 1, s // tk, V_ROWS, tk), lambda bi, h, i: (bi, h, 0, 0, 0)),
                  small, small, small, small, pl.BlockSpec((1, A_VDIM), lambda bi, h, i: (0, 0))],
        out_specs=pl.BlockSpec((1, tq, LANES), lambda bi, h, i: (bi, i, h)),
        out_shape=jax.ShapeDtypeStruct((b, s, A_COLS), BF16),
        scratch_shapes=_flash_scratch(2 * tq, tk),
        compiler_params=_params("parallel", "parallel", "arbitrary"),
        name="diff_attn",
    )(qat, ka, vat, lq1, lk1, lq2, lk2, sw)


def _gqa_kernel(qt_ref, k_ref, vt_ref, o_ref, *scratch, ch):
    tq = qt_ref.shape[3]
    heads_per_chunk = ch // tq

    def q_chunk(c):
        heads = range(c * heads_per_chunk, (c + 1) * heads_per_chunk)
        return jnp.concatenate([qt_ref[0, h] for h in heads], axis=1)

    ot = _flash_t(q_chunk, B_HEADS // heads_per_chunk, ch, k_ref, lambda j: vt_ref[0, j], *scratch)
    chunks = []
    for c in range(B_HEADS // 2):
        rows = slice(((2 * c) // B_GROUP) * HALF, ((2 * c) // B_GROUP + 1) * HALF)
        pair = jnp.concatenate([ot[rows, (2 * c) * tq:(2 * c + 1) * tq],
                                ot[rows, (2 * c + 1) * tq:(2 * c + 2) * tq]], axis=0)
        chunks.append(pair.T)
    o_ref[0] = jnp.concatenate(chunks, axis=1).astype(BF16)


def _gqa_attn(qbt, kb, vbt, tq, tk, ch):
    b, s, _ = kb.shape
    return pl.pallas_call(
        functools.partial(_gqa_kernel, ch=ch),
        grid=(b, s // tq),
        in_specs=[pl.BlockSpec((1, B_HEADS, LANES, tq), lambda bi, i: (bi, 0, 0, i)),
                  pl.BlockSpec((1, s, LANES), lambda bi, i: (bi, 0, 0)),
                  pl.BlockSpec((1, s // tk, V_ROWS, tk), lambda bi, i: (bi, 0, 0, 0))],
        out_specs=pl.BlockSpec((1, tq, B_Q_COLS), lambda bi, i: (bi, i, 0)),
        out_shape=jax.ShapeDtypeStruct((b, s, B_Q_COLS), BF16),
        scratch_shapes=_flash_scratch(B_HEADS * tq, tk),
        compiler_params=_params("parallel", "arbitrary"),
        name="gqa_attn",
    )(qbt, kb, vbt)


def _mix_kernel(oa_ref, ob_ref, x_ref, wo_ref, fnw_ref, wq_ref, keys_ref, x1_ref, xn_ref, sc_ref):
    y = (jnp.dot(oa_ref[...], wo_ref[:A_COLS, :], preferred_element_type=F32)
         + jnp.dot(ob_ref[...], wo_ref[A_COLS:, :], preferred_element_type=F32))
    x1 = x_ref[...] + y
    xn = _rms(x1) * fnw_ref[...]
    x1_ref[...] = x1
    xn_ref[...] = xn
    xb = xn.astype(BF16)
    for hc in range(2 * PEER_HEADS):
        q = jnp.dot(xb, wq_ref[:, hc * PEER_D_HALF:(hc + 1) * PEER_D_HALF], preferred_element_type=F32)
        sc_ref[hc] = lax.dot_general(keys_ref[hc], q.astype(BF16), (((1,), (1,)), ((), ())),
                                     preferred_element_type=F32)


def _mix(oa, ob, x, wo, fnw, wq, keys, tm):
    t, d = x.shape
    nq = wq.shape[1]
    row = lambda n: pl.BlockSpec((tm, n), lambda i: (i, 0))
    full = lambda shape: pl.BlockSpec(shape, lambda i: (0,) * len(shape))
    return pl.pallas_call(
        _mix_kernel,
        grid=(t // tm,),
        in_specs=[row(A_COLS), row(B_Q_COLS), row(d), full((d, d)), full((1, d)), full((d, nq)),
                  full((2 * PEER_HEADS, PEER_N_KEYS, PEER_D_HALF))],
        out_specs=[row(d), row(d),
                   pl.BlockSpec((2 * PEER_HEADS, PEER_N_KEYS, tm), lambda i: (0, 0, i))],
        out_shape=[jax.ShapeDtypeStruct((t, d), F32), jax.ShapeDtypeStruct((t, d), F32),
                   jax.ShapeDtypeStruct((2 * PEER_HEADS, PEER_N_KEYS, t), F32)],
        compiler_params=_params("parallel"),
        name="mix",
    )(oa, ob, x, wo, fnw, wq, keys)


def _topk_rows(x, iota, k):
    n = float(x.shape[0])
    vals, rows = [], []
    for _ in range(k):
        m = jnp.max(x, axis=0, keepdims=True)
        r = jnp.min(jnp.where(x == m, iota, n), axis=0, keepdims=True)
        vals.append(m)
        rows.append(r)
        x = jnp.where(iota == r, -jnp.inf, x)
    return jnp.concatenate(vals, axis=0), jnp.concatenate(rows, axis=0)


def _select_rows(table, sel):
    out = jnp.zeros_like(sel)
    for r in range(table.shape[0]):
        out = jnp.where(sel == float(r), table[r:r + 1, :], out)
    return out


def _topk_kernel(sc_ref, idx_ref, g_ref, idx_sc):
    tt = sc_ref.shape[2]
    iota_keys = lax.broadcasted_iota(jnp.int32, (PEER_N_KEYS, tt), 0).astype(F32)
    iota_cand = lax.broadcasted_iota(jnp.int32, (PEER_TOPK * PEER_TOPK, tt), 0).astype(F32)

    def head(h, carry):
        s1, i1 = _topk_rows(sc_ref[2 * h], iota_keys, PEER_TOPK)
        s2, i2 = _topk_rows(sc_ref[2 * h + 1], iota_keys, PEER_TOPK)
        cand = jnp.concatenate([s1[a:a + 1, :] + s2 for a in range(PEER_TOPK)], axis=0)
        top_s, pos = _topk_rows(cand, iota_cand, PEER_TOPK)
        a = jnp.floor(pos * (1.0 / PEER_TOPK))
        b = pos - a * PEER_TOPK
        expert = _select_rows(i1, a) * PEER_N_KEYS + _select_rows(i2, b)
        e = jnp.exp(top_s - top_s[0:1, :])
        off = pl.multiple_of(h * PEER_TOPK, PEER_TOPK)
        idx_sc[pl.ds(off, PEER_TOPK), :] = expert
        g_ref[pl.ds(off, PEER_TOPK), :] = e / jnp.sum(e, axis=0, keepdims=True)
        return carry

    lax.fori_loop(0, PEER_HEADS, head, 0)
    idx_ref[...] = idx_sc[...].T.astype(jnp.int32)


def _topk(sc, tt):
    t = sc.shape[2]
    return pl.pallas_call(
        _topk_kernel,
        grid=(t // tt,),
        in_specs=[pl.BlockSpec((2 * PEER_HEADS, PEER_N_KEYS, tt), lambda i: (0, 0, i))],
        out_specs=[pl.BlockSpec((tt, PEER_PICKS), lambda i: (i, 0)),
                   pl.BlockSpec((PEER_PICKS, tt), lambda i: (0, i))],
        out_shape=[jax.ShapeDtypeStruct((t, PEER_PICKS), jnp.int32),
                   jax.ShapeDtypeStruct((PEER_PICKS, t), F32)],
        scratch_shapes=[pltpu.VMEM((PEER_PICKS, tt), F32)],
        compiler_params=_params("parallel"),
        name="topk",
    )(sc)


def _gelu(x):
    return 0.5 * x * (1.0 + lax.erf(x * (1.0 / math.sqrt(2.0))))


PEER_SLOTS = 8
PEER_BATCH = 2
PEER_AHEAD = PEER_SLOTS - PEER_BATCH


SUBLANES = 8


def _sublane_fold(vregs):
    sub = lax.broadcasted_iota(jnp.int32, (SUBLANES, LANES), 0)
    level, shift = list(vregs), SUBLANES // 2
    while len(level) > 1:
        keep_first = (sub & shift) == 0
        half = len(level) // 2
        level = [jnp.where(keep_first,
                           level[i] + pltpu.roll(level[i], SUBLANES - shift, axis=0),
                           level[i + half] + pltpu.roll(level[i + half], shift, axis=0))
                 for i in range(half)]
        shift //= 2
    return level[0]


def _peer_kernel(idx_ref, g_ref, xn_ref, x1_ref, fw_ref, uv_hbm, o_ref, *scratch):
    bufs, w_sc, sem = scratch[:PEER_SLOTS], scratch[PEER_SLOTS], scratch[PEER_SLOTS + 1]
    tb = xn_ref.shape[0]
    assert tb % PEER_SLOTS == 0 and tb >= 2 * PEER_SLOTS

    def issue(t, slot):
        for j in range(PEER_PICKS):
            pltpu.make_async_copy(uv_hbm.at[idx_ref[t, j]], bufs[slot].at[j], sem.at[slot]).start()

    def wait(slot):
        pltpu.make_async_copy(uv_hbm.at[pl.ds(0, PEER_PICKS)], bufs[slot], sem.at[slot]).wait()

    lane = lax.broadcasted_iota(jnp.int32, (PEER_PICKS, tb), 1)

    def compute(t, slot):
        buf = bufs[slot]
        x = xn_ref[t]
        acts = []
        for g in range(PEER_PICKS // SUBLANES):
            prods = [buf[g * SUBLANES + k, :, :LANES] * x for k in range(SUBLANES)]
            acts.append(jnp.sum(_sublane_fold(prods), axis=1, keepdims=True))
        act = jnp.concatenate(acts, axis=0)
        gate = jnp.sum(jnp.where(lane == t, g_ref[...], 0.0), axis=1, keepdims=True)
        w_sc[slot] = jnp.broadcast_to(gate * _gelu(act), (PEER_PICKS, LANES))
        partial = [jnp.zeros((SUBLANES, LANES), F32) for _ in range(4)]
        for j in range(PEER_PICKS):
            partial[j % 4] = partial[j % 4] + (jnp.broadcast_to(w_sc[slot, pl.ds(j, 1), :], (SUBLANES, LANES))
                                               * buf[j, :, LANES:])
        r = x1_ref[t] + ((partial[0] + partial[1]) + (partial[2] + partial[3]))
        ms = jnp.sum(jnp.sum(r * r, axis=1, keepdims=True), axis=0, keepdims=True) * (1.0 / D_MODEL)
        o_ref[t] = r * lax.rsqrt(ms + NORM_EPS) * fw_ref[...]

    def group(base, last):
        for s0 in range(0, PEER_SLOTS, PEER_BATCH):
            batch = range(s0, s0 + PEER_BATCH)
            for s in batch:
                wait(s)
            for s in batch:
                if not last or s + PEER_AHEAD < PEER_SLOTS:
                    issue(base + s + PEER_AHEAD, (s + PEER_AHEAD) % PEER_SLOTS)
            for s in batch:
                compute(base + s, s)

    for s in range(PEER_AHEAD):
        issue(s, s)

    def steady(i, carry):
        group(i * PEER_SLOTS, False)
        return carry

    lax.fori_loop(0, tb // PEER_SLOTS - 1, steady, 0)
    group(tb - PEER_SLOTS, True)


def _as_vregs(a):
    return a.reshape(*a.shape[:-1], SUBLANES, LANES)


def _peer(idx, g, xn, x1, fw, uv, tb):
    t, d = xn.shape
    tok = pl.BlockSpec((tb, SUBLANES, LANES), lambda i: (i, 0, 0))
    out = pl.pallas_call(
        _peer_kernel,
        grid=(t // tb,),
        in_specs=[pl.BlockSpec((tb, PEER_PICKS), lambda i: (i, 0), memory_space=pltpu.SMEM),
                  pl.BlockSpec((PEER_PICKS, tb), lambda i: (0, i)),
                  tok, tok, pl.BlockSpec((SUBLANES, LANES), lambda i: (0, 0)),
                  pl.BlockSpec(memory_space=pl.ANY)],
        out_specs=tok,
        out_shape=jax.ShapeDtypeStruct((t, SUBLANES, LANES), F32),
        scratch_shapes=([pltpu.VMEM((PEER_PICKS, SUBLANES, 2 * LANES), F32) for _ in range(PEER_SLOTS)]
                        + [pltpu.VMEM((PEER_SLOTS, PEER_PICKS, LANES), F32),
                           pltpu.SemaphoreType.DMA((PEER_SLOTS,))]),
        compiler_params=_params("arbitrary"),
        name="peer",
    )(idx, g, _as_vregs(xn), _as_vregs(x1), _as_vregs(fw)[0], uv)
    return out.reshape(t, d)


def _forward(x, p):
    b, s, d = x.shape
    tk = min(512, s // 2)
    ch = 2 * LANES
    qat, ka, vat, qbt, kb, vbt = _prep(x, p["anw"], p["w_fused"], _position_tables(s),
                                       p["qnw"], p["qnwp"], p["knw"], p["knwp"], p["ones"], tk)
    oa = _diff_attn(qat, ka, vat, p["lq1"], p["lk1"], p["lq2"], p["lk2"], p["sw"], min(512, s), tk, ch)
    ob = _gqa_attn(qbt, kb, vbt, min(LANES, s), tk, ch)
    t = b * s
    x1, xn, sc = _mix(oa.reshape(t, A_COLS), ob.reshape(t, B_Q_COLS), x.reshape(t, d),
                      p["wo"], p["fnw"], p["wq"], p["keys"], min(256, t))
    idx, g = _topk(sc, LANES)
    y = _peer(idx, g, xn, x1, p["fw"], p["uv"], LANES)
    return y.reshape(b, s, d)


def kernel(x_prompt, x_sample, attn_norm, w_in, lambda_q1, lambda_k1, lambda_q2, lambda_k2, subln_w,
           q_norm_w, k_norm_w, w_out, ffn_norm, peer_wq, peer_keys, peer_u, peer_v, final_norm):
    perm_b, _ = _rotate_half_perm(B_Q_COLS, B_DIM // 2)
    qnw = jnp.tile(q_norm_w[0], B_HEADS)
    knw = jnp.tile(k_norm_w[0], B_KV_HEADS)
    p = dict(
        anw=attn_norm[0][None, :],
        w_fused=_fused_in_weight(w_in[0]),
        qnw=qnw[None, :], qnwp=qnw[perm_b][None, :],
        knw=knw[None, :], knwp=knw[perm_b[:B_KV_COLS]][None, :],
        ones=_group_ones(B_Q_COLS, B_DIM),
        lq1=lambda_q1[0][None, :], lk1=lambda_k1[0][None, :],
        lq2=lambda_q2[0][None, :], lk2=lambda_k2[0][None, :],
        sw=subln_w[0][None, :],
        wo=w_out[0].astype(BF16), fnw=ffn_norm[0][None, :],
        wq=peer_wq[0].astype(BF16),
        keys=peer_keys[0].reshape(2 * PEER_HEADS, PEER_N_KEYS, PEER_D_HALF).astype(BF16),
        uv=jnp.concatenate([_as_vregs(peer_u[0]), _as_vregs(peer_v[0])], axis=2), fw=final_norm[None, :],
    )
    return _forward(x_prompt, p), _forward(x_sample, p)
```

```python
import functools
import math

import numpy as np
import jax
import jax.numpy as jnp
from jax import lax
from jax.experimental import pallas as pl
from jax.experimental.pallas import tpu as pltpu

F32 = jnp.float32
BF16 = jnp.bfloat16

D_MODEL = 1024
A_HEADS = 4
A_DIM = 64
A_VDIM = 2 * A_DIM
B_HEADS = 8
B_KV_HEADS = 2
B_DIM = 64
B_GROUP = B_HEADS // B_KV_HEADS
A_COLS = A_HEADS * 2 * A_DIM
B_Q_COLS = B_HEADS * B_DIM
B_KV_COLS = B_KV_HEADS * B_DIM
ROPE_THETA = 10000.0
GRID_W = 64
NORM_EPS = 1e-6
LAMBDA_INIT = 0.8 - 0.6 * math.exp(-0.3)
PEER_HEADS = 8
PEER_N_KEYS = 128
PEER_D_HALF = 128
PEER_TOPK = 16
PEER_PICKS = PEER_HEADS * PEER_TOPK

LANES = 128
SUBLANES = 8
HALF = LANES // 2
BF16_SUBLANES = 16
V_ROWS = LANES + BF16_SUBLANES
VMEM_LIMIT_BYTES = 56 * 1024 * 1024

OFF_QA, OFF_QA_ROT = 0, 512
OFF_KA, OFF_KA_ROT = 1024, 1536
OFF_VA = 2048
OFF_QB, OFF_QB_ROT = 2560, 3072
OFF_KB, OFF_KB_ROT = 3584, 3712
OFF_VB = 3840
FUSED_COLS = 3968


def _params(*semantics):
    return pltpu.CompilerParams(dimension_semantics=semantics, vmem_limit_bytes=VMEM_LIMIT_BYTES)


def _rope_tables(pos, dim):
    inv = ROPE_THETA ** (-jnp.arange(0, dim, 2, dtype=F32) / dim)
    ang = pos[:, None] * inv[None, :]
    ang = jnp.concatenate([ang, ang], axis=-1)
    return jnp.cos(ang), jnp.sin(ang)


def _position_tables(s):
    rows = s // GRID_W
    t = jnp.arange(s, dtype=F32)
    row = jnp.broadcast_to(jnp.arange(rows, dtype=F32)[:, None], (rows, GRID_W)).reshape(s)
    col = jnp.broadcast_to(jnp.arange(GRID_W, dtype=F32)[None, :], (rows, GRID_W)).reshape(s)
    cos_a, sin_a = _rope_tables(t, A_DIM)
    cos_r, sin_r = _rope_tables(row, B_DIM // 2)
    cos_c, sin_c = _rope_tables(col, B_DIM // 2)
    cos_b = jnp.concatenate([cos_r, cos_c], axis=-1)
    sin_b = jnp.concatenate([sin_r, sin_c], axis=-1)
    two = lambda a: jnp.concatenate([a, a], axis=-1)
    return two(cos_a), two(sin_a), two(cos_b), two(sin_b)


def _rotate_half_perm(n_cols, block):
    i = np.arange(n_cols)
    o = i % block
    half = block // 2
    perm = np.where(o < half, i + half, i - half)
    sign = np.where(o < half, -1.0, 1.0).astype(np.float32)
    return perm, sign


def _fused_in_weight(w_in):
    wqa = w_in[:, 0:512]
    wka = w_in[:, 512:1024]
    wva = w_in[:, 1024:1536]
    wqb = w_in[:, 1536:2048]
    wkb = w_in[:, 2048:2176]
    wvb = w_in[:, 2176:2304]

    def rot(m, block):
        perm, sign = _rotate_half_perm(m.shape[1], block)
        return m[:, perm] * sign

    parts = [wqa, rot(wqa, A_DIM), wka, rot(wka, A_DIM), wva,
             wqb, rot(wqb, B_DIM // 2), wkb, rot(wkb, B_DIM // 2), wvb]
    return jnp.concatenate(parts, axis=1).astype(BF16)


def _group_ones(n, group):
    i = np.arange(n) // group
    return jnp.asarray((i[:, None] == i[None, :]).astype(np.float32), dtype=BF16)


def _rms(x):
    return x * lax.rsqrt(jnp.mean(x * x, axis=-1, keepdims=True) + NORM_EPS)


def _group_mean_sq(p, ones):
    sq = p * p
    hi = sq.astype(BF16)
    lo = (sq - hi.astype(F32)).astype(BF16)
    ss = jnp.dot(hi, ones, preferred_element_type=F32) + jnp.dot(lo, ones, preferred_element_type=F32)
    return ss * (1.0 / B_DIM)


def _prep_kernel(x_ref, anw_ref, w_ref, cosa_ref, sina_ref, cosb_ref, sinb_ref,
                 qnw_ref, qnwp_ref, knw_ref, knwp_ref, ones_ref,
                 qat_ref, ka_ref, vat_ref, qbt_ref, kb_ref, vbt_ref):
    x = x_ref[0]
    tm = x.shape[0]
    hb = (_rms(x) * anw_ref[...]).astype(BF16)

    def proj(off, n):
        return jnp.dot(hb, w_ref[:, off:off + n], preferred_element_type=F32)

    def tile_lanes(a, n):
        return jnp.concatenate([a] * n, axis=1)

    zeros_half = jnp.zeros((HALF, tm), F32)
    ones_rows = jnp.ones((V_ROWS - LANES, tm), BF16)

    cosa = tile_lanes(cosa_ref[...], A_COLS // LANES)
    sina = tile_lanes(sina_ref[...], A_COLS // LANES)
    qa = (proj(OFF_QA, A_COLS) * cosa + proj(OFF_QA_ROT, A_COLS) * sina) * (A_DIM ** -0.5)
    ka = proj(OFF_KA, A_COLS) * cosa + proj(OFF_KA_ROT, A_COLS) * sina
    va = proj(OFF_VA, A_COLS)
    for h in range(A_HEADS):
        qt = qa[:, h * LANES:(h + 1) * LANES].T
        qat_ref[0, 2 * h] = jnp.concatenate([qt[:HALF], zeros_half], axis=0).astype(BF16)
        qat_ref[0, 2 * h + 1] = jnp.concatenate([zeros_half, qt[HALF:]], axis=0).astype(BF16)
        vat_ref[0, h, 0, :LANES, :] = va[:, h * LANES:(h + 1) * LANES].T.astype(BF16)
        vat_ref[0, h, 0, LANES:, :] = ones_rows
    ka_ref[0] = ka.astype(BF16)

    cosb = tile_lanes(cosb_ref[...], B_Q_COLS // LANES)
    sinb = tile_lanes(sinb_ref[...], B_Q_COLS // LANES)
    pq = proj(OFF_QB, B_Q_COLS)
    rq = lax.rsqrt(_group_mean_sq(pq, ones_ref[...]) + NORM_EPS)
    qn = pq * rq * qnw_ref[...]
    qn_rot = proj(OFF_QB_ROT, B_Q_COLS) * rq * qnwp_ref[...]
    qb = (qn * cosb + qn_rot * sinb) * (B_DIM ** -0.5)
    for c in range(B_HEADS // 2):
        qt = qb[:, c * LANES:(c + 1) * LANES].T
        for parity in range(2):
            h = 2 * c + parity
            piece = qt[parity * HALF:(parity + 1) * HALF]
            parts = [piece, zeros_half] if h // B_GROUP == 0 else [zeros_half, piece]
            qbt_ref[0, h] = jnp.concatenate(parts, axis=0).astype(BF16)

    pk = proj(OFF_KB, B_KV_COLS)
    rk = lax.rsqrt(_group_mean_sq(pk, ones_ref[:B_KV_COLS, :B_KV_COLS]) + NORM_EPS)
    kn = pk * rk * knw_ref[...]
    kn_rot = proj(OFF_KB_ROT, B_KV_COLS) * rk * knwp_ref[...]
    kb_ref[0] = (kn * cosb_ref[...] + kn_rot * sinb_ref[...]).astype(BF16)
    vbt_ref[0, 0, :LANES, :] = proj(OFF_VB, B_KV_COLS).T.astype(BF16)
    vbt_ref[0, 0, LANES:, :] = ones_rows


def _prep(x, anw, w_fused, tables, qnw, qnwp, knw, knwp, ones, tm):
    b, s, d = x.shape
    nt = s // tm
    cosa, sina, cosb, sinb = tables
    full = lambda shape: pl.BlockSpec(shape, lambda bi, i: (0,) * len(shape))
    tab = pl.BlockSpec((tm, LANES), lambda bi, i: (i, 0))
    tok = lambda n: pl.BlockSpec((1, tm, n), lambda bi, i: (bi, i, 0))
    qt = lambda heads: pl.BlockSpec((1, heads, LANES, tm), lambda bi, i: (bi, 0, 0, i))
    sds = lambda *shape: jax.ShapeDtypeStruct(shape, BF16)
    return pl.pallas_call(
        _prep_kernel,
        grid=(b, nt),
        in_specs=[pl.BlockSpec((1, tm, d), lambda bi, i: (bi, i, 0)),
                  full((1, d)), full((d, FUSED_COLS)), tab, tab, tab, tab,
                  full((1, B_Q_COLS)), full((1, B_Q_COLS)), full((1, B_KV_COLS)), full((1, B_KV_COLS)),
                  full((B_Q_COLS, B_Q_COLS))],
        out_specs=[qt(2 * A_HEADS), tok(A_COLS),
                   pl.BlockSpec((1, A_HEADS, 1, V_ROWS, tm), lambda bi, i: (bi, 0, i, 0, 0)),
                   qt(B_HEADS), tok(B_KV_COLS),
                   pl.BlockSpec((1, 1, V_ROWS, tm), lambda bi, i: (bi, i, 0, 0))],
        out_shape=[sds(b, 2 * A_HEADS, LANES, s), sds(b, s, A_COLS), sds(b, A_HEADS, nt, V_ROWS, tm),
                   sds(b, B_HEADS, LANES, s), sds(b, s, B_KV_COLS), sds(b, nt, V_ROWS, tm)],
        compiler_params=_params("parallel", "parallel"),
        name="prep",
    )(x, anw, w_fused, cosa, sina, cosb, sinb, qnw, qnwp, knw, knwp, ones)


def _flash_scratch(rows, tk):
    return [pltpu.VMEM((1, rows), F32), pltpu.VMEM((V_ROWS, rows), F32),
            pltpu.VMEM((tk, rows), F32), pltpu.VMEM((tk, rows), F32)]


def _flash_t(q_chunk, n_chunks, ch, k_ref, vt_tile, m_sc, acc_sc, s0_sc, s1_sc):
    m_sc[...] = jnp.full(m_sc.shape, -jnp.inf, F32)
    acc_sc[...] = jnp.zeros(acc_sc.shape, F32)
    tk = s0_sc.shape[0]
    nk = k_ref.shape[1] // tk
    assert nk >= 2 and nk % 2 == 0, "key tiles are processed in pairs"
    chunks = [slice(c * ch, (c + 1) * ch) for c in range(n_chunks)]

    def scores(j, c, dst):
        k = k_ref[0, pl.ds(pl.multiple_of(j * tk, tk), tk), :]
        dst[:, chunks[c]] = jnp.dot(k, q_chunk(c), preferred_element_type=F32)

    def softmax_pv(j, c, src):
        sl = chunks[c]
        s = src[:, sl]
        m_prev = m_sc[:, sl]
        m_new = jnp.maximum(m_prev, jnp.max(s, axis=0, keepdims=True))
        p = jnp.exp(s - m_new).astype(BF16)
        acc_sc[:, sl] = (jnp.exp(m_prev - m_new) * acc_sc[:, sl]
                         + jnp.dot(vt_tile(j), p, preferred_element_type=F32))
        m_sc[:, sl] = m_new

    def step(j, src, dst):
        for c in range(n_chunks):
            if dst is not None:
                scores(j + 1, c, dst)
            softmax_pv(j, c, src)

    for c in range(n_chunks):
        scores(0, c, s0_sc)

    def body(i, carry):
        step(2 * i, s0_sc, s1_sc)
        step(2 * i + 1, s1_sc, s0_sc)
        return carry

    lax.fori_loop(0, nk // 2 - 1, body, 0)
    step(nk - 2, s0_sc, s1_sc)
    step(nk - 1, s1_sc, None)
    acc = acc_sc[...]
    return acc[:LANES] / acc[LANES:LANES + 1]


def _diff_attn_kernel(qt_ref, k_ref, vt_ref, lq1_ref, lk1_ref, lq2_ref, lk2_ref, sw_ref,
                      o_ref, *scratch, ch):
    tq = qt_ref.shape[3]
    per_map = tq // ch

    def q_chunk(c):
        return qt_ref[0, c // per_map, :, (c % per_map) * ch:(c % per_map + 1) * ch]

    ot = _flash_t(q_chunk, 2 * per_map, ch, k_ref, lambda j: vt_ref[0, 0, j], *scratch)
    lam = (jnp.exp(jnp.sum(lq1_ref[...] * lk1_ref[...], axis=1, keepdims=True))
           - jnp.exp(jnp.sum(lq2_ref[...] * lk2_ref[...], axis=1, keepdims=True)) + LAMBDA_INIT)
    d = (ot[:, :tq] - lam * ot[:, tq:]).T
    o_ref[0] = (_rms(d) * sw_ref[...] * (1.0 - LAMBDA_INIT)).astype(BF16)


def _diff_attn(qat, ka, vat, lq1, lk1, lq2, lk2, sw, tq, tk, ch):
    b, s, _ = ka.shape
    small = pl.BlockSpec((1, A_DIM), lambda bi, h, i: (0, 0))
    return pl.pallas_call(
        functools.partial(_diff_attn_kernel, ch=ch),
        grid=(b, A_HEADS, s // tq),
        in_specs=[pl.BlockSpec((1, 2, LANES, tq), lambda bi, h, i: (bi, h, 0, i)),
                  pl.BlockSpec((1, s, LANES), lambda bi, h, i: (bi, 0, h)),
                  pl.BlockSpec((1, 1, s // tk, V_ROWS, tk), lambda bi, h, i: (bi, h, 0, 0, 0)),
                  small, small, small, small, pl.BlockSpec((1, A_VDIM), lambda bi, h, i: (0, 0))],
        out_specs=pl.BlockSpec((1, tq, LANES), lambda bi, h, i: (bi, i, h)),
        out_shape=jax.ShapeDtypeStruct((b, s, A_COLS), BF16),
        scratch_shapes=_flash_scratch(2 * tq, tk),
        compiler_params=_params("parallel", "parallel", "arbitrary"),
        name="diff_attn",
    )(qat, ka, vat, lq1, lk1, lq2, lk2, sw)


def _gqa_kernel(qt_ref, k_ref, vt_ref, o_ref, *scratch, ch):
    tq = qt_ref.shape[3]
    heads_per_chunk = ch // tq

    def q_chunk(c):
        heads = range(c * heads_per_chunk, (c + 1) * heads_per_chunk)
        return jnp.concatenate([qt_ref[0, h] for h in heads], axis=1)

    ot = _flash_t(q_chunk, B_HEADS // heads_per_chunk, ch, k_ref, lambda j: vt_ref[0, j], *scratch)
    chunks = []
    for c in range(B_HEADS // 2):
        rows = slice(((2 * c) // B_GROUP) * HALF, ((2 * c) // B_GROUP + 1) * HALF)
        pair = jnp.concatenate([ot[rows, (2 * c) * tq:(2 * c + 1) * tq],
                                ot[rows, (2 * c + 1) * tq:(2 * c + 2) * tq]], axis=0)
        chunks.append(pair.T)
    o_ref[0] = jnp.concatenate(chunks, axis=1).astype(BF16)


def _gqa_attn(qbt, kb, vbt, tq, tk, ch):
    b, s, _ = kb.shape
    return pl.pallas_call(
        functools.partial(_gqa_kernel, ch=ch),
        grid=(b, s // tq),
        in_specs=[pl.BlockSpec((1, B_HEADS, LANES, tq), lambda bi, i: (bi, 0, 0, i)),
                  pl.BlockSpec((1, s, LANES), lambda bi, i: (bi, 0, 0)),
                  pl.BlockSpec((1, s // tk, V_ROWS, tk), lambda bi, i: (bi, 0, 0, 0))],
        out_specs=pl.BlockSpec((1, tq, B_Q_COLS), lambda bi, i: (bi, i, 0)),
        out_shape=jax.ShapeDtypeStruct((b, s, B_Q_COLS), BF16),
        scratch_shapes=_flash_scratch(B_HEADS * tq, tk),
        compiler_params=_params("parallel", "arbitrary"),
        name="gqa_attn",
    )(qbt, kb, vbt)


def _mix_kernel(oa_ref, ob_ref, x_ref, wo_ref, fnw_ref, wq_ref, keys_ref, x1_ref, xn_ref, sc_ref):
    y = (jnp.dot(oa_ref[...], wo_ref[:A_COLS, :], preferred_element_type=F32)
         + jnp.dot(ob_ref[...], wo_ref[A_COLS:, :], preferred_element_type=F32))
    x1 = x_ref[...] + y
    xn = _rms(x1) * fnw_ref[...]
    x1_ref[...] = x1
    xn_ref[...] = xn
    xb = xn.astype(BF16)
    for hc in range(2 * PEER_HEADS):
        q = jnp.dot(xb, wq_ref[:, hc * PEER_D_HALF:(hc + 1) * PEER_D_HALF], preferred_element_type=F32)
        sc_ref[hc] = lax.dot_general(keys_ref[hc], q.astype(BF16), (((1,), (1,)), ((), ())),
                                     preferred_element_type=F32)


def _mix(oa, ob, x, wo, fnw, wq, keys, tm):
    t, d = x.shape
    nq = wq.shape[1]
    row = lambda n: pl.BlockSpec((tm, n), lambda i: (i, 0))
    full = lambda shape: pl.BlockSpec(shape, lambda i: (0,) * len(shape))
    return pl.pallas_call(
        _mix_kernel,
        grid=(t // tm,),
        in_specs=[row(A_COLS), row(B_Q_COLS), row(d), full((d, d)), full((1, d)), full((d, nq)),
                  full((2 * PEER_HEADS, PEER_N_KEYS, PEER_D_HALF))],
        out_specs=[row(d), row(d),
                   pl.BlockSpec((2 * PEER_HEADS, PEER_N_KEYS, tm), lambda i: (0, 0, i))],
        out_shape=[jax.ShapeDtypeStruct((t, d), F32), jax.ShapeDtypeStruct((t, d), F32),
                   jax.ShapeDtypeStruct((2 * PEER_HEADS, PEER_N_KEYS, t), F32)],
        compiler_params=_params("parallel"),
        name="mix",
    )(oa, ob, x, wo, fnw, wq, keys)


_NO_ROW = float(2 ** 20)


def _topk_rows(x, iota, k):
    vals, rows = [], []
    for _ in range(k):
        m = jnp.max(x, axis=0, keepdims=True)
        r = jnp.min(jnp.where(x == m, iota, _NO_ROW), axis=0, keepdims=True)
        vals.append(m)
        rows.append(r)
        x = jnp.where(iota == r, -jnp.inf, x)
    return jnp.concatenate(vals, axis=0), jnp.concatenate(rows, axis=0)


def _select_rows(table, sel):
    out = jnp.zeros_like(sel)
    for r in range(table.shape[0]):
        out = jnp.where(sel == float(r), table[r:r + 1, :], out)
    return out


def _pair_candidates(s1, s2, sub):
    lo, hi = s2[:SUBLANES], s2[SUBLANES:]
    vals, ids = [s1[0:1] + lo, s1[0:1] + hi], [sub, sub + float(SUBLANES)]
    for a in range(1, SUBLANES):
        bound = PEER_TOPK // (a + 1)
        v = s1[a:a + 1] + lo
        vals.append(v if bound >= SUBLANES else jnp.where(sub < float(bound), v, -jnp.inf))
        ids.append(sub + float(a * PEER_TOPK))
    vals.append(s1[SUBLANES:] + s2[0:1])
    ids.append((sub + float(SUBLANES)) * float(PEER_TOPK))
    return jnp.concatenate(vals, axis=0), jnp.concatenate(ids, axis=0)


def _topk_kernel(sc_ref, idx_ref, g_ref, idx_sc):
    tt = sc_ref.shape[2]
    iota_keys = lax.broadcasted_iota(jnp.int32, (PEER_N_KEYS, tt), 0).astype(F32)
    sub = lax.broadcasted_iota(jnp.int32, (SUBLANES, tt), 0).astype(F32)

    def head(h, carry):
        s1, i1 = _topk_rows(sc_ref[2 * h], iota_keys, PEER_TOPK)
        s2, i2 = _topk_rows(sc_ref[2 * h + 1], iota_keys, PEER_TOPK)
        top_s, pos = _topk_rows(*_pair_candidates(s1, s2, sub), PEER_TOPK)
        a = jnp.floor(pos * (1.0 / PEER_TOPK))
        b = pos - a * PEER_TOPK
        expert = _select_rows(i1, a) * PEER_N_KEYS + _select_rows(i2, b)
        e = jnp.exp(top_s - top_s[0:1, :])
        off = pl.multiple_of(h * PEER_TOPK, PEER_TOPK)
        idx_sc[pl.ds(off, PEER_TOPK), :] = expert
        g_ref[pl.ds(off, PEER_TOPK), :] = e / jnp.sum(e, axis=0, keepdims=True)
        return carry

    lax.fori_loop(0, PEER_HEADS, head, 0)
    idx_ref[...] = idx_sc[...].T.astype(jnp.int32)


def _topk(sc, tt):
    t = sc.shape[2]
    return pl.pallas_call(
        _topk_kernel,
        grid=(t // tt,),
        in_specs=[pl.BlockSpec((2 * PEER_HEADS, PEER_N_KEYS, tt), lambda i: (0, 0, i))],
        out_specs=[pl.BlockSpec((tt, PEER_PICKS), lambda i: (i, 0)),
                   pl.BlockSpec((PEER_PICKS, tt), lambda i: (0, i))],
        out_shape=[jax.ShapeDtypeStruct((t, PEER_PICKS), jnp.int32),
                   jax.ShapeDtypeStruct((PEER_PICKS, t), F32)],
        scratch_shapes=[pltpu.VMEM((PEER_PICKS, tt), F32)],
        compiler_params=_params("parallel"),
        name="topk",
    )(sc)


def _gelu(x):
    return 0.5 * x * (1.0 + lax.erf(x * (1.0 / math.sqrt(2.0))))


PEER_SLOTS = 8
PEER_BATCH = 2
PEER_AHEAD = PEER_SLOTS - PEER_BATCH


def _sublane_fold(vregs):
    sub = lax.broadcasted_iota(jnp.int32, (SUBLANES, LANES), 0)
    level, shift = list(vregs), SUBLANES // 2
    while len(level) > 1:
        keep_first = (sub & shift) == 0
        half = len(level) // 2
        level = [jnp.where(keep_first,
                           level[i] + pltpu.roll(level[i], SUBLANES - shift, axis=0),
                           level[i + half] + pltpu.roll(level[i + half], shift, axis=0))
                 for i in range(half)]
        shift //= 2
    return level[0]


def _peer_kernel(idx_ref, g_ref, xn_ref, x1_ref, fw_ref, uv_hbm, o_ref, *scratch):
    bufs, w_sc, sem = scratch[:PEER_SLOTS], scratch[PEER_SLOTS], scratch[PEER_SLOTS + 1]
    tb = xn_ref.shape[0]
    assert tb % PEER_SLOTS == 0 and tb >= 2 * PEER_SLOTS

    def issue(t, slot):
        for j in range(PEER_PICKS):
            pltpu.make_async_copy(uv_hbm.at[idx_ref[t, j]], bufs[slot].at[j], sem.at[slot]).start(priority=j % 2)

    def wait(slot):
        pltpu.make_async_copy(uv_hbm.at[pl.ds(0, PEER_PICKS)], bufs[slot], sem.at[slot]).wait()

    lane = lax.broadcasted_iota(jnp.int32, (PEER_PICKS, tb), 1)

    def compute(t, slot):
        buf = bufs[slot]
        x = xn_ref[t]
        acts = []
        for g in range(PEER_PICKS // SUBLANES):
            prods = [buf[g * SUBLANES + k, :, :LANES] * x for k in range(SUBLANES)]
            acts.append(jnp.sum(_sublane_fold(prods), axis=1, keepdims=True))
        act = jnp.concatenate(acts, axis=0)
        gate = jnp.sum(jnp.where(lane == t, g_ref[...], 0.0), axis=1, keepdims=True)
        w_sc[slot] = jnp.broadcast_to(gate * _gelu(act), (PEER_PICKS, LANES))
        partial = [jnp.zeros((SUBLANES, LANES), F32) for _ in range(4)]
        for j in range(PEER_PICKS):
            partial[j % 4] = partial[j % 4] + (jnp.broadcast_to(w_sc[slot, pl.ds(j, 1), :], (SUBLANES, LANES))
                                               * buf[j, :, LANES:])
        r = x1_ref[t] + ((partial[0] + partial[1]) + (partial[2] + partial[3]))
        ms = jnp.sum(jnp.sum(r * r, axis=1, keepdims=True), axis=0, keepdims=True) * (1.0 / D_MODEL)
        o_ref[t] = r * lax.rsqrt(ms + NORM_EPS) * fw_ref[...]

    def group(base, last):
        for s0 in range(0, PEER_SLOTS, PEER_BATCH):
            batch = range(s0, s0 + PEER_BATCH)
            for s in batch:
                wait(s)
            for s in batch:
                if not last or s + PEER_AHEAD < PEER_SLOTS:
                    issue(base + s + PEER_AHEAD, (s + PEER_AHEAD) % PEER_SLOTS)
            for s in batch:
                compute(base + s, s)

    for s in range(PEER_AHEAD):
        issue(s, s)

    def steady(i, carry):
        group(i * PEER_SLOTS, False)
        return carry

    lax.fori_loop(0, tb // PEER_SLOTS - 1, steady, 0)
    group(tb - PEER_SLOTS, True)


def _as_vregs(a):
    return a.reshape(*a.shape[:-1], SUBLANES, LANES)


def _peer(idx, g, xn, x1, fw, uv, tb):
    t, d = xn.shape
    tok = pl.BlockSpec((tb, SUBLANES, LANES), lambda i: (i, 0, 0))
    out = pl.pallas_call(
        _peer_kernel,
        grid=(t // tb,),
        in_specs=[pl.BlockSpec((tb, PEER_PICKS), lambda i: (i, 0), memory_space=pltpu.SMEM),
                  pl.BlockSpec((PEER_PICKS, tb), lambda i: (0, i)),
                  tok, tok, pl.BlockSpec((SUBLANES, LANES), lambda i: (0, 0)),
                  pl.BlockSpec(memory_space=pl.ANY)],
        out_specs=tok,
        out_shape=jax.ShapeDtypeStruct((t, SUBLANES, LANES), F32),
        scratch_shapes=([pltpu.VMEM((PEER_PICKS, SUBLANES, 2 * LANES), F32) for _ in range(PEER_SLOTS)]
                        + [pltpu.VMEM((PEER_SLOTS, PEER_PICKS, LANES), F32),
                           pltpu.SemaphoreType.DMA((PEER_SLOTS,))]),
        compiler_params=_params("arbitrary"),
        name="peer",
    )(idx, g, _as_vregs(xn), _as_vregs(x1), _as_vregs(fw)[0], uv)
    return out.reshape(t, d)


def _forward(x, p):
    b, s, d = x.shape
    tk = min(512, s // 2)
    ch = 2 * LANES
    qat, ka, vat, qbt, kb, vbt = _prep(x, p["anw"], p["w_fused"], _position_tables(s),
                                       p["qnw"], p["qnwp"], p["knw"], p["knwp"], p["ones"], tk)
    oa = _diff_attn(qat, ka, vat, p["lq1"], p["lk1"], p["lq2"], p["lk2"], p["sw"], min(512, s), tk, ch)
    ob = _gqa_attn(qbt, kb, vbt, min(LANES, s), tk, ch)
    t = b * s
    x1, xn, sc = _mix(oa.reshape(t, A_COLS), ob.reshape(t, B_Q_COLS), x.reshape(t, d),
                      p["wo"], p["fnw"], p["wq"], p["keys"], min(256, t))
    idx, g = _topk(sc, LANES)
    y = _peer(idx, g, xn, x1, p["fw"], p["uv"], min(2 * LANES, t))
    return y.reshape(b, s, d)


def kernel(x_prompt, x_sample, attn_norm, w_in, lambda_q1, lambda_k1, lambda_q2, lambda_k2, subln_w,
           q_norm_w, k_norm_w, w_out, ffn_norm, peer_wq, peer_keys, peer_u, peer_v, final_norm):
    perm_b, _ = _rotate_half_perm(B_Q_COLS, B_DIM // 2)
    qnw = jnp.tile(q_norm_w[0], B_HEADS)
    knw = jnp.tile(k_norm_w[0], B_KV_HEADS)
    p = dict(
        anw=attn_norm[0][None, :],
        w_fused=_fused_in_weight(w_in[0]),
        qnw=qnw[None, :], qnwp=qnw[perm_b][None, :],
        knw=knw[None, :], knwp=knw[perm_b[:B_KV_COLS]][None, :],
        ones=_group_ones(B_Q_COLS, B_DIM),
        lq1=lambda_q1[0][None, :], lk1=lambda_k1[0][None, :],
        lq2=lambda_q2[0][None, :], lk2=lambda_k2[0][None, :],
        sw=subln_w[0][None, :],
        wo=w_out[0].astype(BF16), fnw=ffn_norm[0][None, :],
        wq=peer_wq[0].astype(BF16),
        keys=peer_keys[0].reshape(2 * PEER_HEADS, PEER_N_KEYS, PEER_D_HALF).astype(BF16),
        uv=jnp.concatenate([_as_vregs(peer_u[0]), _as_vregs(peer_v[0])], axis=2), fw=final_norm[None, :],
    )
    return _forward(x_prompt, p), _forward(x_sample, p)
```

```python
import functools
import math

import numpy as np
import jax
import jax.numpy as jnp
from jax import lax
from jax.experimental import pallas as pl
from jax.experimental.pallas import tpu as pltpu

F32 = jnp.float32
BF16 = jnp.bfloat16

D_MODEL = 1024
A_HEADS = 4
A_DIM = 64
A_VDIM = 2 * A_DIM
B_HEADS = 8
B_KV_HEADS = 2
B_DIM = 64
B_GROUP = B_HEADS // B_KV_HEADS
A_COLS = A_HEADS * 2 * A_DIM
B_Q_COLS = B_HEADS * B_DIM
B_KV_COLS = B_KV_HEADS * B_DIM
ROPE_THETA = 10000.0
GRID_W = 64
NORM_EPS = 1e-6
LAMBDA_INIT = 0.8 - 0.6 * math.exp(-0.3)
PEER_HEADS = 8
PEER_N_KEYS = 128
PEER_D_HALF = 128
PEER_TOPK = 16
PEER_PICKS = PEER_HEADS * PEER_TOPK

LANES = 128
SUBLANES = 8
HALF = LANES // 2
BF16_SUBLANES = 16
ONES_ROWS = BF16_SUBLANES
V_ROWS_A = A_VDIM + ONES_ROWS
V_ROWS_B = B_DIM + ONES_ROWS
VMEM_LIMIT_BYTES = 56 * 1024 * 1024

OFF_QA, OFF_QA_ROT = 0, 512
OFF_KA, OFF_KA_ROT = 1024, 1536
OFF_VA = 2048
OFF_QB, OFF_QB_ROT = 2560, 3072
OFF_KB, OFF_KB_ROT = 3584, 3712
OFF_VB = 3840
FUSED_COLS = 3968


def _params(*semantics):
    return pltpu.CompilerParams(dimension_semantics=semantics, vmem_limit_bytes=VMEM_LIMIT_BYTES)


def _rope_tables(pos, dim):
    inv = ROPE_THETA ** (-jnp.arange(0, dim, 2, dtype=F32) / dim)
    ang = pos[:, None] * inv[None, :]
    ang = jnp.concatenate([ang, ang], axis=-1)
    return jnp.cos(ang), jnp.sin(ang)


def _position_tables(s):
    rows = s // GRID_W
    t = jnp.arange(s, dtype=F32)
    row = jnp.broadcast_to(jnp.arange(rows, dtype=F32)[:, None], (rows, GRID_W)).reshape(s)
    col = jnp.broadcast_to(jnp.arange(GRID_W, dtype=F32)[None, :], (rows, GRID_W)).reshape(s)
    cos_a, sin_a = _rope_tables(t, A_DIM)
    cos_r, sin_r = _rope_tables(row, B_DIM // 2)
    cos_c, sin_c = _rope_tables(col, B_DIM // 2)
    cos_b = jnp.concatenate([cos_r, cos_c], axis=-1)
    sin_b = jnp.concatenate([sin_r, sin_c], axis=-1)
    two = lambda a: jnp.concatenate([a, a], axis=-1)
    return two(cos_a), two(sin_a), two(cos_b), two(sin_b)


def _rotate_half_perm(n_cols, block):
    i = np.arange(n_cols)
    o = i % block
    half = block // 2
    perm = np.where(o < half, i + half, i - half)
    sign = np.where(o < half, -1.0, 1.0).astype(np.float32)
    return perm, sign


def _fused_in_weight(w_in):
    wqa = w_in[:, 0:512]
    wka = w_in[:, 512:1024]
    wva = w_in[:, 1024:1536]
    wqb = w_in[:, 1536:2048]
    wkb = w_in[:, 2048:2176]
    wvb = w_in[:, 2176:2304]

    def rot(m, block):
        perm, sign = _rotate_half_perm(m.shape[1], block)
        return m[:, perm] * sign

    parts = [wqa, rot(wqa, A_DIM), wka, rot(wka, A_DIM), wva,
             wqb, rot(wqb, B_DIM // 2), wkb, rot(wkb, B_DIM // 2), wvb]
    return jnp.concatenate(parts, axis=1).astype(BF16)


def _group_ones(n, group):
    i = np.arange(n) // group
    return jnp.asarray((i[:, None] == i[None, :]).astype(np.float32), dtype=BF16)


def _rms(x):
    return x * lax.rsqrt(jnp.mean(x * x, axis=-1, keepdims=True) + NORM_EPS)


def _group_mean_sq(p, ones):
    sq = p * p
    hi = sq.astype(BF16)
    lo = (sq - hi.astype(F32)).astype(BF16)
    ss = jnp.dot(hi, ones, preferred_element_type=F32) + jnp.dot(lo, ones, preferred_element_type=F32)
    return ss * (1.0 / B_DIM)


def _prep_kernel(x_ref, anw_ref, w_ref, cosa_ref, sina_ref, cosb_ref, sinb_ref,
                 qnw_ref, qnwp_ref, knw_ref, knwp_ref, ones_ref,
                 qat_ref, ka_ref, vat_ref, qbt_ref, kb_ref, vbt_ref):
    x = x_ref[0]
    tm = x.shape[0]
    hb = (_rms(x) * anw_ref[...]).astype(BF16)

    def proj(off, n):
        return jnp.dot(hb, w_ref[:, off:off + n], preferred_element_type=F32)

    def tile_lanes(a, n):
        return jnp.concatenate([a] * n, axis=1)

    zeros_half = jnp.zeros((HALF, tm), F32)
    ones_rows = jnp.ones((ONES_ROWS, tm), BF16)

    cosa = tile_lanes(cosa_ref[...], A_COLS // LANES)
    sina = tile_lanes(sina_ref[...], A_COLS // LANES)
    qa = (proj(OFF_QA, A_COLS) * cosa + proj(OFF_QA_ROT, A_COLS) * sina) * (A_DIM ** -0.5)
    ka = proj(OFF_KA, A_COLS) * cosa + proj(OFF_KA_ROT, A_COLS) * sina
    va = proj(OFF_VA, A_COLS)
    for h in range(A_HEADS):
        qt = qa[:, h * LANES:(h + 1) * LANES].T
        qat_ref[0, 2 * h] = jnp.concatenate([qt[:HALF], zeros_half], axis=0).astype(BF16)
        qat_ref[0, 2 * h + 1] = jnp.concatenate([zeros_half, qt[HALF:]], axis=0).astype(BF16)
        vat_ref[0, h, 0, :LANES, :] = va[:, h * LANES:(h + 1) * LANES].T.astype(BF16)
        vat_ref[0, h, 0, LANES:, :] = ones_rows
    ka_ref[0] = ka.astype(BF16)

    cosb = tile_lanes(cosb_ref[...], B_Q_COLS // LANES)
    sinb = tile_lanes(sinb_ref[...], B_Q_COLS // LANES)
    pq = proj(OFF_QB, B_Q_COLS)
    rq = lax.rsqrt(_group_mean_sq(pq, ones_ref[...]) + NORM_EPS)
    qn = pq * rq * qnw_ref[...]
    qn_rot = proj(OFF_QB_ROT, B_Q_COLS) * rq * qnwp_ref[...]
    qb = (qn * cosb + qn_rot * sinb) * (B_DIM ** -0.5)
    for c in range(B_HEADS // 2):
        qt = qb[:, c * LANES:(c + 1) * LANES].T
        for parity in range(2):
            h = 2 * c + parity
            piece = qt[parity * HALF:(parity + 1) * HALF]
            parts = [piece, zeros_half] if h // B_GROUP == 0 else [zeros_half, piece]
            qbt_ref[0, h] = jnp.concatenate(parts, axis=0).astype(BF16)

    assert OFF_KB_ROT == OFF_KB + B_KV_COLS
    pk2 = proj(OFF_KB, 2 * B_KV_COLS)
    pk = pk2[:, :B_KV_COLS]
    rk = lax.rsqrt(_group_mean_sq(pk, ones_ref[:B_KV_COLS, :B_KV_COLS]) + NORM_EPS)
    kn = pk * rk * knw_ref[...]
    kn_rot = pk2[:, B_KV_COLS:] * rk * knwp_ref[...]
    kb_ref[0] = (kn * cosb_ref[...] + kn_rot * sinb_ref[...]).astype(BF16)
    vbt = proj(OFF_VB, B_KV_COLS).T.astype(BF16)
    for g in range(B_KV_HEADS):
        vbt_ref[0, 0, g, :B_DIM, :] = vbt[g * B_DIM:(g + 1) * B_DIM]
        vbt_ref[0, 0, g, B_DIM:, :] = ones_rows


def _prep(x, anw, w_fused, tables, qnw, qnwp, knw, knwp, ones, tm):
    b, s, d = x.shape
    nt = s // tm
    cosa, sina, cosb, sinb = tables
    full = lambda shape: pl.BlockSpec(shape, lambda bi, i: (0,) * len(shape))
    tab = pl.BlockSpec((tm, LANES), lambda bi, i: (i, 0))
    tok = lambda n: pl.BlockSpec((1, tm, n), lambda bi, i: (bi, i, 0))
    qt = lambda heads: pl.BlockSpec((1, heads, LANES, tm), lambda bi, i: (bi, 0, 0, i))
    sds = lambda *shape: jax.ShapeDtypeStruct(shape, BF16)
    return pl.pallas_call(
        _prep_kernel,
        grid=(b, nt),
        in_specs=[pl.BlockSpec((1, tm, d), lambda bi, i: (bi, i, 0)),
                  full((1, d)), full((d, FUSED_COLS)), tab, tab, tab, tab,
                  full((1, B_Q_COLS)), full((1, B_Q_COLS)), full((1, B_KV_COLS)), full((1, B_KV_COLS)),
                  full((B_Q_COLS, B_Q_COLS))],
        out_specs=[qt(2 * A_HEADS), tok(A_COLS),
                   pl.BlockSpec((1, A_HEADS, 1, V_ROWS_A, tm), lambda bi, i: (bi, 0, i, 0, 0)),
                   qt(B_HEADS), tok(B_KV_COLS),
                   pl.BlockSpec((1, 1, B_KV_HEADS, V_ROWS_B, tm), lambda bi, i: (bi, i, 0, 0, 0))],
        out_shape=[sds(b, 2 * A_HEADS, LANES, s), sds(b, s, A_COLS), sds(b, A_HEADS, nt, V_ROWS_A, tm),
                   sds(b, B_HEADS, LANES, s), sds(b, s, B_KV_COLS), sds(b, nt, B_KV_HEADS, V_ROWS_B, tm)],
        compiler_params=_params("parallel", "parallel"),
        name="prep",
    )(x, anw, w_fused, cosa, sina, cosb, sinb, qnw, qnwp, knw, knwp, ones)


def _flash_scratch(rows, tk, v_rows):
    return [pltpu.VMEM((1, rows), F32), pltpu.VMEM((v_rows, rows), F32),
            pltpu.VMEM((tk, rows), F32), pltpu.VMEM((tk, rows), F32)]


def _flash_t(q_chunk, n_chunks, ch, k_ref, vt_tile, m_sc, acc_sc, s0_sc, s1_sc):
    m_sc[...] = jnp.full(m_sc.shape, -jnp.inf, F32)
    acc_sc[...] = jnp.zeros(acc_sc.shape, F32)
    tk = s0_sc.shape[0]
    nk = k_ref.shape[1] // tk
    assert nk >= 2 and nk % 2 == 0, "key tiles are processed in pairs"
    chunks = [slice(c * ch, (c + 1) * ch) for c in range(n_chunks)]

    def scores(j, c, dst):
        k = k_ref[0, pl.ds(pl.multiple_of(j * tk, tk), tk), :]
        dst[:, chunks[c]] = jnp.dot(k, q_chunk(c), preferred_element_type=F32)

    def softmax_pv(j, c, src):
        sl = chunks[c]
        s = src[:, sl]
        m_prev = m_sc[:, sl]
        m_new = jnp.maximum(m_prev, jnp.max(s, axis=0, keepdims=True))
        p = jnp.exp(s - m_new).astype(BF16)
        acc_sc[:, sl] = (jnp.exp(m_prev - m_new) * acc_sc[:, sl]
                         + jnp.dot(vt_tile(j, c), p, preferred_element_type=F32))
        m_sc[:, sl] = m_new

    def step(j, src, dst):
        for c in range(n_chunks):
            if dst is not None:
                scores(j + 1, c, dst)
            softmax_pv(j, c, src)

    for c in range(n_chunks):
        scores(0, c, s0_sc)

    def body(i, carry):
        step(2 * i, s0_sc, s1_sc)
        step(2 * i + 1, s1_sc, s0_sc)
        return carry

    lax.fori_loop(0, nk // 2 - 1, body, 0)
    step(nk - 2, s0_sc, s1_sc)
    step(nk - 1, s1_sc, None)
    acc = acc_sc[...]
    n_val = acc.shape[0] - ONES_ROWS
    return acc[:n_val] / acc[n_val:n_val + 1]


def _diff_attn_kernel(qt_ref, k_ref, vt_ref, lq1_ref, lk1_ref, lq2_ref, lk2_ref, sw_ref,
                      o_ref, *scratch, ch):
    tq = qt_ref.shape[3]
    per_map = tq // ch

    def q_chunk(c):
        return qt_ref[0, c // per_map, :, (c % per_map) * ch:(c % per_map + 1) * ch]

    ot = _flash_t(q_chunk, 2 * per_map, ch, k_ref, lambda j, c: vt_ref[0, 0, j], *scratch)
    lam = (jnp.exp(jnp.sum(lq1_ref[...] * lk1_ref[...], axis=1, keepdims=True))
           - jnp.exp(jnp.sum(lq2_ref[...] * lk2_ref[...], axis=1, keepdims=True)) + LAMBDA_INIT)
    d = (ot[:, :tq] - lam * ot[:, tq:]).T
    o_ref[0] = (_rms(d) * sw_ref[...] * (1.0 - LAMBDA_INIT)).astype(BF16)


def _diff_attn(qat, ka, vat, lq1, lk1, lq2, lk2, sw, tq, tk, ch):
    b, s, _ = ka.shape
    small = pl.BlockSpec((1, A_DIM), lambda bi, h, i: (0, 0))
    return pl.pallas_call(
        functools.partial(_diff_attn_kernel, ch=ch),
        grid=(b, A_HEADS, s // tq),
        in_specs=[pl.BlockSpec((1, 2, LANES, tq), lambda bi, h, i: (bi, h, 0, i)),
                  pl.BlockSpec((1, s, LANES), lambda bi, h, i: (bi, 0, h)),
                  pl.BlockSpec((1, 1, s // tk, V_ROWS_A, tk), lambda bi, h, i: (bi, h, 0, 0, 0)),
                  small, small, small, small, pl.BlockSpec((1, A_VDIM), lambda bi, h, i: (0, 0))],
        out_specs=pl.BlockSpec((1, tq, LANES), lambda bi, h, i: (bi, i, h)),
        out_shape=jax.ShapeDtypeStruct((b, s, A_COLS), BF16),
        scratch_shapes=_flash_scratch(2 * tq, tk, V_ROWS_A),
        compiler_params=_params("parallel", "parallel", "arbitrary"),
        name="diff_attn",
    )(qat, ka, vat, lq1, lk1, lq2, lk2, sw)


def _gqa_kernel(qt_ref, k_ref, vt_ref, o_ref, *scratch, ch):
    tq = qt_ref.shape[3]
    heads_per_chunk = ch // tq

    def q_chunk(c):
        heads = range(c * heads_per_chunk, (c + 1) * heads_per_chunk)
        return jnp.concatenate([qt_ref[0, h] for h in heads], axis=1)

    def vt_tile(j, c):
        return vt_ref[0, j, (c * heads_per_chunk) // B_GROUP]

    ot = _flash_t(q_chunk, B_HEADS // heads_per_chunk, ch, k_ref, vt_tile, *scratch)
    pairs = [ot[:, (2 * c) * tq:(2 * c + 2) * tq] for c in range(B_HEADS // 2)]
    o_ref[0] = jnp.concatenate(
        [jnp.concatenate([p[:, :tq], p[:, tq:]], axis=0).T for p in pairs], axis=1).astype(BF16)


def _gqa_attn(qbt, kb, vbt, tq, tk, ch):
    b, s, _ = kb.shape
    return pl.pallas_call(
        functools.partial(_gqa_kernel, ch=ch),
        grid=(b, s // tq),
        in_specs=[pl.BlockSpec((1, B_HEADS, LANES, tq), lambda bi, i: (bi, 0, 0, i)),
                  pl.BlockSpec((1, s, LANES), lambda bi, i: (bi, 0, 0)),
                  pl.BlockSpec((1, s // tk, B_KV_HEADS, V_ROWS_B, tk), lambda bi, i: (bi, 0, 0, 0, 0))],
        out_specs=pl.BlockSpec((1, tq, B_Q_COLS), lambda bi, i: (bi, i, 0)),
        out_shape=jax.ShapeDtypeStruct((b, s, B_Q_COLS), BF16),
        scratch_shapes=_flash_scratch(B_HEADS * tq, tk, V_ROWS_B),
        compiler_params=_params("parallel", "arbitrary"),
        name="gqa_attn",
    )(qbt, kb, vbt)


def _mix_kernel(oa_ref, ob_ref, x_ref, wo_ref, fnw_ref, wq_ref, keys_ref, x1_ref, xn_ref, sc_ref):
    y = (jnp.dot(oa_ref[...], wo_ref[:A_COLS, :], preferred_element_type=F32)
         + jnp.dot(ob_ref[...], wo_ref[A_COLS:, :], preferred_element_type=F32))
    x1 = x_ref[...] + y
    xn = _rms(x1) * fnw_ref[...]
    x1_ref[...] = x1
    xn_ref[...] = xn
    xb = xn.astype(BF16)
    for h in range(PEER_HEADS):
        q = jnp.dot(xb, wq_ref[:, h * 2 * PEER_D_HALF:(h + 1) * 2 * PEER_D_HALF],
                    preferred_element_type=F32).astype(BF16)
        for c in range(2):
            sc_ref[2 * h + c] = lax.dot_general(
                keys_ref[2 * h + c], q[:, c * PEER_D_HALF:(c + 1) * PEER_D_HALF], (((1,), (1,)), ((), ())),
                preferred_element_type=F32)


def _mix(oa, ob, x, wo, fnw, wq, keys, tm):
    t, d = x.shape
    nq = wq.shape[1]
    row = lambda n: pl.BlockSpec((tm, n), lambda i: (i, 0))
    full = lambda shape: pl.BlockSpec(shape, lambda i: (0,) * len(shape))
    return pl.pallas_call(
        _mix_kernel,
        grid=(t // tm,),
        in_specs=[row(A_COLS), row(B_Q_COLS), row(d), full((d, d)), full((1, d)), full((d, nq)),
                  full((2 * PEER_HEADS, PEER_N_KEYS, PEER_D_HALF))],
        out_specs=[row(d), row(d),
                   pl.BlockSpec((2 * PEER_HEADS, PEER_N_KEYS, tm), lambda i: (0, 0, i))],
        out_shape=[jax.ShapeDtypeStruct((t, d), F32), jax.ShapeDtypeStruct((t, d), F32),
                   jax.ShapeDtypeStruct((2 * PEER_HEADS, PEER_N_KEYS, t), F32)],
        compiler_params=_params("parallel"),
        name="mix",
    )(oa, ob, x, wo, fnw, wq, keys)


_NO_ROW = float(2 ** 20)


def _topk_rows(x, iota, k):
    vals, rows = [], []
    for _ in range(k):
        m = jnp.max(x, axis=0, keepdims=True)
        r = jnp.min(jnp.where(x == m, iota, _NO_ROW), axis=0, keepdims=True)
        vals.append(m)
        rows.append(r)
        x = jnp.where(iota == r, -jnp.inf, x)
    return jnp.concatenate(vals, axis=0), jnp.concatenate(rows, axis=0)


def _select_rows(table, sel):
    out = jnp.zeros_like(sel)
    for r in range(table.shape[0]):
        out = jnp.where(sel == float(r), table[r:r + 1, :], out)
    return out


def _pair_candidates(s1, s2, sub):
    lo, hi = s2[:SUBLANES], s2[SUBLANES:]
    vals, ids = [s1[0:1] + lo, s1[0:1] + hi], [sub, sub + float(SUBLANES)]
    for a in range(1, SUBLANES):
        bound = PEER_TOPK // (a + 1)
        v = s1[a:a + 1] + lo
        vals.append(v if bound >= SUBLANES else jnp.where(sub < float(bound), v, -jnp.inf))
        ids.append(sub + float(a * PEER_TOPK))
    vals.append(s1[SUBLANES:] + s2[0:1])
    ids.append((sub + float(SUBLANES)) * float(PEER_TOPK))
    return jnp.concatenate(vals, axis=0), jnp.concatenate(ids, axis=0)


def _topk_kernel(sc_ref, idx_ref, g_ref, idx_sc):
    tt = sc_ref.shape[2]
    iota_keys = lax.broadcasted_iota(jnp.int32, (PEER_N_KEYS, tt), 0).astype(F32)
    sub = lax.broadcasted_iota(jnp.int32, (SUBLANES, tt), 0).astype(F32)

    def head(h, carry):
        s1, i1 = _topk_rows(sc_ref[2 * h], iota_keys, PEER_TOPK)
        s2, i2 = _topk_rows(sc_ref[2 * h + 1], iota_keys, PEER_TOPK)
        top_s, pos = _topk_rows(*_pair_candidates(s1, s2, sub), PEER_TOPK)
        a = jnp.floor(pos * (1.0 / PEER_TOPK))
        b = pos - a * PEER_TOPK
        expert = _select_rows(i1, a) * PEER_N_KEYS + _select_rows(i2, b)
        e = jnp.exp(top_s - top_s[0:1, :])
        off = pl.multiple_of(h * PEER_TOPK, PEER_TOPK)
        idx_sc[pl.ds(off, PEER_TOPK), :] = expert
        g_ref[pl.ds(off, PEER_TOPK), :] = e / jnp.sum(e, axis=0, keepdims=True)
        return carry

    lax.fori_loop(0, PEER_HEADS, head, 0)
    idx_ref[...] = idx_sc[...].T.astype(jnp.int32)


def _topk(sc, tt):
    t = sc.shape[2]
    return pl.pallas_call(
        _topk_kernel,
        grid=(t // tt,),
        in_specs=[pl.BlockSpec((2 * PEER_HEADS, PEER_N_KEYS, tt), lambda i: (0, 0, i))],
        out_specs=[pl.BlockSpec((tt, PEER_PICKS), lambda i: (i, 0)),
                   pl.BlockSpec((PEER_PICKS, tt), lambda i: (0, i))],
        out_shape=[jax.ShapeDtypeStruct((t, PEER_PICKS), jnp.int32),
                   jax.ShapeDtypeStruct((PEER_PICKS, t), F32)],
        scratch_shapes=[pltpu.VMEM((PEER_PICKS, tt), F32)],
        compiler_params=_params("parallel"),
        name="topk",
    )(sc)


def _gelu(x):
    return 0.5 * x * (1.0 + lax.erf(x * (1.0 / math.sqrt(2.0))))


PEER_SLOTS = 8
PEER_BATCH = 2
PEER_AHEAD = PEER_SLOTS - PEER_BATCH


def _sublane_fold(vregs):
    sub = lax.broadcasted_iota(jnp.int32, (SUBLANES, LANES), 0)
    level, shift = list(vregs), SUBLANES // 2
    while len(level) > 1:
        keep_first = (sub & shift) == 0
        half = len(level) // 2
        level = [jnp.where(keep_first,
                           level[i] + pltpu.roll(level[i], SUBLANES - shift, axis=0),
                           level[i + half] + pltpu.roll(level[i + half], shift, axis=0))
                 for i in range(half)]
        shift //= 2
    return level[0]


def _peer_kernel(idx_ref, g_ref, xn_ref, x1_ref, fw_ref, uv_hbm, o_ref, *scratch):
    bufs, w_sc, sem = scratch[:PEER_SLOTS], scratch[PEER_SLOTS], scratch[PEER_SLOTS + 1]
    tb = xn_ref.shape[0]
    assert tb % PEER_SLOTS == 0 and tb >= 2 * PEER_SLOTS

    def issue(t, slot):
        for j in range(PEER_PICKS):
            pltpu.make_async_copy(uv_hbm.at[idx_ref[t, j]], bufs[slot].at[j], sem.at[slot]).start(priority=j % 2)

    def wait(slot):
        pltpu.make_async_copy(uv_hbm.at[pl.ds(0, PEER_PICKS)], bufs[slot], sem.at[slot]).wait()

    lane = lax.broadcasted_iota(jnp.int32, (PEER_PICKS, tb), 1)

    def compute(t, slot):
        buf = bufs[slot]
        x = xn_ref[t]
        acts = []
        for g in range(PEER_PICKS // SUBLANES):
            prods = [buf[g * SUBLANES + k, :, :LANES] * x for k in range(SUBLANES)]
            acts.append(jnp.sum(_sublane_fold(prods), axis=1, keepdims=True))
        act = jnp.concatenate(acts, axis=0)
        gate = jnp.sum(jnp.where(lane == t, g_ref[...], 0.0), axis=1, keepdims=True)
        w_sc[slot] = jnp.broadcast_to(gate * _gelu(act), (PEER_PICKS, LANES))
        partial = [jnp.zeros((SUBLANES, LANES), F32) for _ in range(4)]
        for j in range(PEER_PICKS):
            partial[j % 4] = partial[j % 4] + (jnp.broadcast_to(w_sc[slot, pl.ds(j, 1), :], (SUBLANES, LANES))
                                               * buf[j, :, LANES:])
        r = x1_ref[t] + ((partial[0] + partial[1]) + (partial[2] + partial[3]))
        ms = jnp.sum(jnp.sum(r * r, axis=1, keepdims=True), axis=0, keepdims=True) * (1.0 / D_MODEL)
        o_ref[t] = r * lax.rsqrt(ms + NORM_EPS) * fw_ref[...]

    def group(base, last):
        for s0 in range(0, PEER_SLOTS, PEER_BATCH):
            batch = range(s0, s0 + PEER_BATCH)
            for s in batch:
                wait(s)
            for s in batch:
                if not last or s + PEER_AHEAD < PEER_SLOTS:
                    issue(base + s + PEER_AHEAD, (s + PEER_AHEAD) % PEER_SLOTS)
            for s in batch:
                compute(base + s, s)

    for s in range(PEER_AHEAD):
        issue(s, s)

    def steady(i, carry):
        group(i * PEER_SLOTS, False)
        return carry

    lax.fori_loop(0, tb // PEER_SLOTS - 1, steady, 0)
    group(tb - PEER_SLOTS, True)


def _as_vregs(a):
    return a.reshape(*a.shape[:-1], SUBLANES, LANES)


def _peer(idx, g, xn, x1, fw, uv, tb):
    t, d = xn.shape
    tok = pl.BlockSpec((tb, SUBLANES, LANES), lambda i: (i, 0, 0))
    out = pl.pallas_call(
        _peer_kernel,
        grid=(t // tb,),
        in_specs=[pl.BlockSpec((tb, PEER_PICKS), lambda i: (i, 0), memory_space=pltpu.SMEM),
                  pl.BlockSpec((PEER_PICKS, tb), lambda i: (0, i)),
                  tok, tok, pl.BlockSpec((SUBLANES, LANES), lambda i: (0, 0)),
                  pl.BlockSpec(memory_space=pl.ANY)],
        out_specs=tok,
        out_shape=jax.ShapeDtypeStruct((t, SUBLANES, LANES), F32),
        scratch_shapes=([pltpu.VMEM((PEER_PICKS, SUBLANES, 2 * LANES), F32) for _ in range(PEER_SLOTS)]
                        + [pltpu.VMEM((PEER_SLOTS, PEER_PICKS, LANES), F32),
                           pltpu.SemaphoreType.DMA((PEER_SLOTS,))]),
        compiler_params=_params("arbitrary"),
        name="peer",
    )(idx, g, _as_vregs(xn), _as_vregs(x1), _as_vregs(fw)[0], uv)
    return out.reshape(t, d)


def _forward(x, p):
    b, s, d = x.shape
    tk = min(512, s // 2)
    ch = 2 * LANES
    qat, ka, vat, qbt, kb, vbt = _prep(x, p["anw"], p["w_fused"], _position_tables(s),
                                       p["qnw"], p["qnwp"], p["knw"], p["knwp"], p["ones"], tk)
    oa = _diff_attn(qat, ka, vat, p["lq1"], p["lk1"], p["lq2"], p["lk2"], p["sw"], min(512, s), tk, ch)
    ob = _gqa_attn(qbt, kb, vbt, min(LANES, s), tk, ch)
    t = b * s
    x1, xn, sc = _mix(oa.reshape(t, A_COLS), ob.reshape(t, B_Q_COLS), x.reshape(t, d),
                      p["wo"], p["fnw"], p["wq"], p["keys"], min(512, t))
    idx, g = _topk(sc, LANES)
    y = _peer(idx, g, xn, x1, p["fw"], p["uv"], min(2 * LANES, t))
    return y.reshape(b, s, d)


def kernel(x_prompt, x_sample, attn_norm, w_in, lambda_q1, lambda_k1, lambda_q2, lambda_k2, subln_w,
           q_norm_w, k_norm_w, w_out, ffn_norm, peer_wq, peer_keys, peer_u, peer_v, final_norm):
    perm_b, _ = _rotate_half_perm(B_Q_COLS, B_DIM // 2)
    qnw = jnp.tile(q_norm_w[0], B_HEADS)
    knw = jnp.tile(k_norm_w[0], B_KV_HEADS)
    p = dict(
        anw=attn_norm[0][None, :],
        w_fused=_fused_in_weight(w_in[0]),
        qnw=qnw[None, :], qnwp=qnw[perm_b][None, :],
        knw=knw[None, :], knwp=knw[perm_b[:B_KV_COLS]][None, :],
        ones=_group_ones(B_Q_COLS, B_DIM),
        lq1=lambda_q1[0][None, :], lk1=lambda_k1[0][None, :],
        lq2=lambda_q2[0][None, :], lk2=lambda_k2[0][None, :],
        sw=subln_w[0][None, :],
        wo=w_out[0].astype(BF16), fnw=ffn_norm[0][None, :],
        wq=peer_wq[0].astype(BF16),
        keys=peer_keys[0].reshape(2 * PEER_HEADS, PEER_N_KEYS, PEER_D_HALF).astype(BF16),
        uv=jnp.concatenate([_as_vregs(peer_u[0]), _as_vregs(peer_v[0])], axis=2), fw=final_norm[None, :],
    )
    return _forward(x_prompt, p), _forward(x_sample, p)
```

```python
import functools
import math

import numpy as np
import jax
import jax.numpy as jnp
from jax import lax
from jax.experimental import pallas as pl
from jax.experimental.pallas import tpu as pltpu

F32 = jnp.float32
BF16 = jnp.bfloat16

D_MODEL = 1024
A_HEADS = 4
A_DIM = 64
A_VDIM = 2 * A_DIM
B_HEADS = 8
B_KV_HEADS = 2
B_DIM = 64
B_GROUP = B_HEADS // B_KV_HEADS
A_COLS = A_HEADS * 2 * A_DIM
B_Q_COLS = B_HEADS * B_DIM
B_KV_COLS = B_KV_HEADS * B_DIM
ROPE_THETA = 10000.0
GRID_W = 64
NORM_EPS = 1e-6
LAMBDA_INIT = 0.8 - 0.6 * math.exp(-0.3)
PEER_HEADS = 8
PEER_N_KEYS = 128
PEER_D_HALF = 128
PEER_TOPK = 16
PEER_PICKS = PEER_HEADS * PEER_TOPK

LANES = 128
SUBLANES = 8
HALF = LANES // 2
BF16_SUBLANES = 16
ONES_ROWS = BF16_SUBLANES
V_ROWS_A = A_VDIM + ONES_ROWS
V_ROWS_B = B_DIM + ONES_ROWS
VMEM_LIMIT_BYTES = 56 * 1024 * 1024

OFF_QA, OFF_QA_ROT = 0, 512
OFF_KA, OFF_KA_ROT = 1024, 1536
OFF_VA = 2048
OFF_QB, OFF_QB_ROT = 2560, 3072
OFF_KB, OFF_KB_ROT = 3584, 3712
OFF_VB = 3840
FUSED_COLS = 3968


def _params(*semantics):
    return pltpu.CompilerParams(dimension_semantics=semantics, vmem_limit_bytes=VMEM_LIMIT_BYTES)


def _rope_tables(pos, dim):
    inv = ROPE_THETA ** (-jnp.arange(0, dim, 2, dtype=F32) / dim)
    ang = pos[:, None] * inv[None, :]
    ang = jnp.concatenate([ang, ang], axis=-1)
    return jnp.cos(ang), jnp.sin(ang)


def _position_tables(s):
    rows = s // GRID_W
    t = jnp.arange(s, dtype=F32)
    row = jnp.broadcast_to(jnp.arange(rows, dtype=F32)[:, None], (rows, GRID_W)).reshape(s)
    col = jnp.broadcast_to(jnp.arange(GRID_W, dtype=F32)[None, :], (rows, GRID_W)).reshape(s)
    cos_a, sin_a = _rope_tables(t, A_DIM)
    cos_r, sin_r = _rope_tables(row, B_DIM // 2)
    cos_c, sin_c = _rope_tables(col, B_DIM // 2)
    cos_b = jnp.concatenate([cos_r, cos_c], axis=-1)
    sin_b = jnp.concatenate([sin_r, sin_c], axis=-1)
    two = lambda a: jnp.concatenate([a, a], axis=-1)
    return two(cos_a), two(sin_a), two(cos_b), two(sin_b)


def _rotate_half_perm(n_cols, block):
    i = np.arange(n_cols)
    o = i % block
    half = block // 2
    perm = np.where(o < half, i + half, i - half)
    sign = np.where(o < half, -1.0, 1.0).astype(np.float32)
    return perm, sign


def _fused_in_weight(w_in):
    wqa = w_in[:, 0:512]
    wka = w_in[:, 512:1024]
    wva = w_in[:, 1024:1536]
    wqb = w_in[:, 1536:2048]
    wkb = w_in[:, 2048:2176]
    wvb = w_in[:, 2176:2304]

    def rot(m, block):
        perm, sign = _rotate_half_perm(m.shape[1], block)
        return m[:, perm] * sign

    parts = [wqa, rot(wqa, A_DIM), wka, rot(wka, A_DIM), wva,
             wqb, rot(wqb, B_DIM // 2), wkb, rot(wkb, B_DIM // 2), wvb]
    return jnp.concatenate(parts, axis=1).astype(BF16)


def _group_ones(n, group):
    i = np.arange(n) // group
    return jnp.asarray((i[:, None] == i[None, :]).astype(np.float32), dtype=BF16)


def _rms(x):
    return x * lax.rsqrt(jnp.mean(x * x, axis=-1, keepdims=True) + NORM_EPS)


def _group_mean_sq(p, ones):
    sq = p * p
    hi = sq.astype(BF16)
    lo = (sq - hi.astype(F32)).astype(BF16)
    ss = jnp.dot(hi, ones, preferred_element_type=F32) + jnp.dot(lo, ones, preferred_element_type=F32)
    return ss * (1.0 / B_DIM)


def _prep_kernel(x_ref, anw_ref, w_ref, cosa_ref, sina_ref, cosb_ref, sinb_ref,
                 qnw_ref, qnwp_ref, knw_ref, knwp_ref, ones_ref,
                 qat_ref, ka_ref, vat_ref, qbt_ref, kb_ref, vbt_ref):
    x = x_ref[0]
    tm = x.shape[0]
    hb = (_rms(x) * anw_ref[...]).astype(BF16)

    def proj(off, n):
        return jnp.dot(hb, w_ref[:, off:off + n], preferred_element_type=F32)

    def tile_lanes(a, n):
        return jnp.concatenate([a] * n, axis=1)

    zeros_half = jnp.zeros((HALF, tm), F32)
    ones_rows = jnp.ones((ONES_ROWS, tm), BF16)

    cosa = tile_lanes(cosa_ref[...], A_COLS // LANES)
    sina = tile_lanes(sina_ref[...], A_COLS // LANES)
    qa = (proj(OFF_QA, A_COLS) * cosa + proj(OFF_QA_ROT, A_COLS) * sina) * (A_DIM ** -0.5)
    ka = proj(OFF_KA, A_COLS) * cosa + proj(OFF_KA_ROT, A_COLS) * sina
    va = proj(OFF_VA, A_COLS)
    for h in range(A_HEADS):
        qt = qa[:, h * LANES:(h + 1) * LANES].T
        qat_ref[0, 2 * h] = jnp.concatenate([qt[:HALF], zeros_half], axis=0).astype(BF16)
        qat_ref[0, 2 * h + 1] = jnp.concatenate([zeros_half, qt[HALF:]], axis=0).astype(BF16)
        vat_ref[0, h, 0, :LANES, :] = va[:, h * LANES:(h + 1) * LANES].T.astype(BF16)
        vat_ref[0, h, 0, LANES:, :] = ones_rows
    ka_ref[0] = ka.astype(BF16)

    cosb = tile_lanes(cosb_ref[...], B_Q_COLS // LANES)
    sinb = tile_lanes(sinb_ref[...], B_Q_COLS // LANES)
    pq = proj(OFF_QB, B_Q_COLS)
    rq = lax.rsqrt(_group_mean_sq(pq, ones_ref[...]) + NORM_EPS)
    qn = pq * rq * qnw_ref[...]
    qn_rot = proj(OFF_QB_ROT, B_Q_COLS) * rq * qnwp_ref[...]
    qb = (qn * cosb + qn_rot * sinb) * (B_DIM ** -0.5)
    for c in range(B_HEADS // 2):
        qt = qb[:, c * LANES:(c + 1) * LANES].T
        for parity in range(2):
            h = 2 * c + parity
            piece = qt[parity * HALF:(parity + 1) * HALF]
            parts = [piece, zeros_half] if h // B_GROUP == 0 else [zeros_half, piece]
            qbt_ref[0, h] = jnp.concatenate(parts, axis=0).astype(BF16)

    assert OFF_KB_ROT == OFF_KB + B_KV_COLS
    pk2 = proj(OFF_KB, 2 * B_KV_COLS)
    pk = pk2[:, :B_KV_COLS]
    rk = lax.rsqrt(_group_mean_sq(pk, ones_ref[:B_KV_COLS, :B_KV_COLS]) + NORM_EPS)
    kn = pk * rk * knw_ref[...]
    kn_rot = pk2[:, B_KV_COLS:] * rk * knwp_ref[...]
    kb_ref[0] = (kn * cosb_ref[...] + kn_rot * sinb_ref[...]).astype(BF16)
    vbt = proj(OFF_VB, B_KV_COLS).T.astype(BF16)
    for g in range(B_KV_HEADS):
        vbt_ref[0, 0, g, :B_DIM, :] = vbt[g * B_DIM:(g + 1) * B_DIM]
        vbt_ref[0, 0, g, B_DIM:, :] = ones_rows


def _prep(x, anw, w_fused, tables, qnw, qnwp, knw, knwp, ones, tm):
    b, s, d = x.shape
    nt = s // tm
    cosa, sina, cosb, sinb = tables
    full = lambda shape: pl.BlockSpec(shape, lambda bi, i: (0,) * len(shape))
    tab = pl.BlockSpec((tm, LANES), lambda bi, i: (i, 0))
    tok = lambda n: pl.BlockSpec((1, tm, n), lambda bi, i: (bi, i, 0))
    qt = lambda heads: pl.BlockSpec((1, heads, LANES, tm), lambda bi, i: (bi, 0, 0, i))
    sds = lambda *shape: jax.ShapeDtypeStruct(shape, BF16)
    return pl.pallas_call(
        _prep_kernel,
        grid=(b, nt),
        in_specs=[pl.BlockSpec((1, tm, d), lambda bi, i: (bi, i, 0)),
                  full((1, d)), full((d, FUSED_COLS)), tab, tab, tab, tab,
                  full((1, B_Q_COLS)), full((1, B_Q_COLS)), full((1, B_KV_COLS)), full((1, B_KV_COLS)),
                  full((B_Q_COLS, B_Q_COLS))],
        out_specs=[qt(2 * A_HEADS), tok(A_COLS),
                   pl.BlockSpec((1, A_HEADS, 1, V_ROWS_A, tm), lambda bi, i: (bi, 0, i, 0, 0)),
                   qt(B_HEADS), tok(B_KV_COLS),
                   pl.BlockSpec((1, 1, B_KV_HEADS, V_ROWS_B, tm), lambda bi, i: (bi, i, 0, 0, 0))],
        out_shape=[sds(b, 2 * A_HEADS, LANES, s), sds(b, s, A_COLS), sds(b, A_HEADS, nt, V_ROWS_A, tm),
                   sds(b, B_HEADS, LANES, s), sds(b, s, B_KV_COLS), sds(b, nt, B_KV_HEADS, V_ROWS_B, tm)],
        compiler_params=_params("parallel", "parallel"),
        name="prep",
    )(x, anw, w_fused, cosa, sina, cosb, sinb, qnw, qnwp, knw, knwp, ones)


def _flash_scratch(rows, tk, v_rows):
    return [pltpu.VMEM((1, rows), F32), pltpu.VMEM((v_rows, rows), F32),
            pltpu.VMEM((tk, rows), F32), pltpu.VMEM((tk, rows), F32)]


def _flash_t(q_chunk, n_chunks, ch, k_ref, vt_tile, m_sc, acc_sc, s0_sc, s1_sc):
    m_sc[...] = jnp.full(m_sc.shape, -jnp.inf, F32)
    acc_sc[...] = jnp.zeros(acc_sc.shape, F32)
    tk = s0_sc.shape[0]
    nk = k_ref.shape[1] // tk
    assert nk >= 2 and nk % 2 == 0, "key tiles are processed in pairs"
    chunks = [slice(c * ch, (c + 1) * ch) for c in range(n_chunks)]

    def scores(j, c, dst):
        k = k_ref[0, pl.ds(pl.multiple_of(j * tk, tk), tk), :]
        dst[:, chunks[c]] = jnp.dot(k, q_chunk(c), preferred_element_type=F32)

    def softmax_pv(j, c, src):
        sl = chunks[c]
        s = src[:, sl]
        m_prev = m_sc[:, sl]
        m_new = jnp.maximum(m_prev, jnp.max(s, axis=0, keepdims=True))
        p = jnp.exp(s - m_new).astype(BF16)
        acc_sc[:, sl] = (jnp.exp(m_prev - m_new) * acc_sc[:, sl]
                         + jnp.dot(vt_tile(j, c), p, preferred_element_type=F32))
        m_sc[:, sl] = m_new

    def step(j, src, dst):
        for c in range(n_chunks):
            if dst is not None:
                scores(j + 1, c, dst)
            softmax_pv(j, c, src)

    for c in range(n_chunks):
        scores(0, c, s0_sc)

    def body(i, carry):
        step(2 * i, s0_sc, s1_sc)
        step(2 * i + 1, s1_sc, s0_sc)
        return carry

    lax.fori_loop(0, nk // 2 - 1, body, 0)
    step(nk - 2, s0_sc, s1_sc)
    step(nk - 1, s1_sc, None)
    acc = acc_sc[...]
    n_val = acc.shape[0] - ONES_ROWS
    return acc[:n_val] / acc[n_val:n_val + 1]


def _diff_attn_kernel(qt_ref, k_ref, vt_ref, lq1_ref, lk1_ref, lq2_ref, lk2_ref, sw_ref,
                      o_ref, *scratch, ch):
    tq = qt_ref.shape[3]
    per_map = tq // ch

    def q_chunk(c):
        return qt_ref[0, c // per_map, :, (c % per_map) * ch:(c % per_map + 1) * ch]

    ot = _flash_t(q_chunk, 2 * per_map, ch, k_ref, lambda j, c: vt_ref[0, 0, j], *scratch)
    lam = (jnp.exp(jnp.sum(lq1_ref[...] * lk1_ref[...], axis=1, keepdims=True))
           - jnp.exp(jnp.sum(lq2_ref[...] * lk2_ref[...], axis=1, keepdims=True)) + LAMBDA_INIT)
    d = (ot[:, :tq] - lam * ot[:, tq:]).T
    o_ref[0] = (_rms(d) * sw_ref[...] * (1.0 - LAMBDA_INIT)).astype(BF16)


def _diff_attn(qat, ka, vat, lq1, lk1, lq2, lk2, sw, tq, tk, ch):
    b, s, _ = ka.shape
    small = pl.BlockSpec((1, A_DIM), lambda bi, h, i: (0, 0))
    return pl.pallas_call(
        functools.partial(_diff_attn_kernel, ch=ch),
        grid=(b, A_HEADS, s // tq),
        in_specs=[pl.BlockSpec((1, 2, LANES, tq), lambda bi, h, i: (bi, h, 0, i)),
                  pl.BlockSpec((1, s, LANES), lambda bi, h, i: (bi, 0, h)),
                  pl.BlockSpec((1, 1, s // tk, V_ROWS_A, tk), lambda bi, h, i: (bi, h, 0, 0, 0)),
                  small, small, small, small, pl.BlockSpec((1, A_VDIM), lambda bi, h, i: (0, 0))],
        out_specs=pl.BlockSpec((1, tq, LANES), lambda bi, h, i: (bi, i, h)),
        out_shape=jax.ShapeDtypeStruct((b, s, A_COLS), BF16),
        scratch_shapes=_flash_scratch(2 * tq, tk, V_ROWS_A),
        compiler_params=_params("parallel", "parallel", "arbitrary"),
        name="diff_attn",
    )(qat, ka, vat, lq1, lk1, lq2, lk2, sw)


def _gqa_kernel(qt_ref, k_ref, vt_ref, o_ref, *scratch, ch):
    tq = qt_ref.shape[3]
    heads_per_chunk = ch // tq

    def q_chunk(c):
        heads = range(c * heads_per_chunk, (c + 1) * heads_per_chunk)
        return jnp.concatenate([qt_ref[0, h] for h in heads], axis=1)

    def vt_tile(j, c):
        return vt_ref[0, j, (c * heads_per_chunk) // B_GROUP]

    ot = _flash_t(q_chunk, B_HEADS // heads_per_chunk, ch, k_ref, vt_tile, *scratch)
    pairs = [ot[:, (2 * c) * tq:(2 * c + 2) * tq] for c in range(B_HEADS // 2)]
    o_ref[0] = jnp.concatenate(
        [jnp.concatenate([p[:, :tq], p[:, tq:]], axis=0).T for p in pairs], axis=1).astype(BF16)


def _gqa_attn(qbt, kb, vbt, tq, tk, ch):
    b, s, _ = kb.shape
    return pl.pallas_call(
        functools.partial(_gqa_kernel, ch=ch),
        grid=(b, s // tq),
        in_specs=[pl.BlockSpec((1, B_HEADS, LANES, tq), lambda bi, i: (bi, 0, 0, i)),
                  pl.BlockSpec((1, s, LANES), lambda bi, i: (bi, 0, 0)),
                  pl.BlockSpec((1, s // tk, B_KV_HEADS, V_ROWS_B, tk), lambda bi, i: (bi, 0, 0, 0, 0))],
        out_specs=pl.BlockSpec((1, tq, B_Q_COLS), lambda bi, i: (bi, i, 0)),
        out_shape=jax.ShapeDtypeStruct((b, s, B_Q_COLS), BF16),
        scratch_shapes=_flash_scratch(B_HEADS * tq, tk, V_ROWS_B),
        compiler_params=_params("parallel", "arbitrary"),
        name="gqa_attn",
    )(qbt, kb, vbt)


def _mix_kernel(oa_ref, ob_ref, x_ref, wo_ref, fnw_ref, wq_ref, keys_ref, x1_ref, xn_ref, sc_ref):
    y = (jnp.dot(oa_ref[...], wo_ref[:A_COLS, :], preferred_element_type=F32)
         + jnp.dot(ob_ref[...], wo_ref[A_COLS:, :], preferred_element_type=F32))
    x1 = x_ref[...] + y
    xn = _rms(x1) * fnw_ref[...]
    x1_ref[...] = x1
    xn_ref[...] = xn
    xb = xn.astype(BF16)
    for h in range(PEER_HEADS):
        q = jnp.dot(xb, wq_ref[:, h * 2 * PEER_D_HALF:(h + 1) * 2 * PEER_D_HALF],
                    preferred_element_type=F32).astype(BF16)
        for c in range(2):
            sc_ref[2 * h + c] = lax.dot_general(
                keys_ref[2 * h + c], q[:, c * PEER_D_HALF:(c + 1) * PEER_D_HALF], (((1,), (1,)), ((), ())),
                preferred_element_type=F32)


def _mix(oa, ob, x, wo, fnw, wq, keys, tm):
    t, d = x.shape
    nq = wq.shape[1]
    row = lambda n: pl.BlockSpec((tm, n), lambda i: (i, 0))
    full = lambda shape: pl.BlockSpec(shape, lambda i: (0,) * len(shape))
    return pl.pallas_call(
        _mix_kernel,
        grid=(t // tm,),
        in_specs=[row(A_COLS), row(B_Q_COLS), row(d), full((d, d)), full((1, d)), full((d, nq)),
                  full((2 * PEER_HEADS, PEER_N_KEYS, PEER_D_HALF))],
        out_specs=[row(d), row(d),
                   pl.BlockSpec((2 * PEER_HEADS, PEER_N_KEYS, tm), lambda i: (0, 0, i))],
        out_shape=[jax.ShapeDtypeStruct((t, d), F32), jax.ShapeDtypeStruct((t, d), F32),
                   jax.ShapeDtypeStruct((2 * PEER_HEADS, PEER_N_KEYS, t), F32)],
        compiler_params=_params("parallel"),
        name="mix",
    )(oa, ob, x, wo, fnw, wq, keys)


_NO_ROW = float(2 ** 20)


def _topk_rows(x, iota, k):
    vals, rows = [], []
    for _ in range(k):
        m = jnp.max(x, axis=0, keepdims=True)
        r = jnp.min(jnp.where(x == m, iota, _NO_ROW), axis=0, keepdims=True)
        vals.append(m)
        rows.append(r)
        x = jnp.where(iota == r, -jnp.inf, x)
    return jnp.concatenate(vals, axis=0), jnp.concatenate(rows, axis=0)


def _select_rows(table, sel):
    out = jnp.zeros_like(sel)
    for r in range(table.shape[0]):
        out = jnp.where(sel == float(r), table[r:r + 1, :], out)
    return out


def _pair_candidates(s1, s2, sub):
    lo, hi = s2[:SUBLANES], s2[SUBLANES:]
    vals, ids = [s1[0:1] + lo, s1[0:1] + hi], [sub, sub + float(SUBLANES)]
    for a in range(1, SUBLANES):
        bound = PEER_TOPK // (a + 1)
        v = s1[a:a + 1] + lo
        vals.append(v if bound >= SUBLANES else jnp.where(sub < float(bound), v, -jnp.inf))
        ids.append(sub + float(a * PEER_TOPK))
    vals.append(s1[SUBLANES:] + s2[0:1])
    ids.append((sub + float(SUBLANES)) * float(PEER_TOPK))
    return jnp.concatenate(vals, axis=0), jnp.concatenate(ids, axis=0)


def _topk_kernel(sc_ref, idx_ref, g_ref, idx_sc):
    tt = sc_ref.shape[2]
    iota_keys = lax.broadcasted_iota(jnp.int32, (PEER_N_KEYS, tt), 0).astype(F32)
    sub = lax.broadcasted_iota(jnp.int32, (SUBLANES, tt), 0).astype(F32)

    def key_stage(h):
        s1, i1 = _topk_rows(sc_ref[2 * h], iota_keys, PEER_TOPK)
        s2, i2 = _topk_rows(sc_ref[2 * h + 1], iota_keys, PEER_TOPK)
        return s1, i1, s2, i2

    def pair_stage(h, s1, i1, s2, i2):
        top_s, pos = _topk_rows(*_pair_candidates(s1, s2, sub), PEER_TOPK)
        a = jnp.floor(pos * (1.0 / PEER_TOPK))
        b = pos - a * PEER_TOPK
        expert = _select_rows(i1, a) * PEER_N_KEYS + _select_rows(i2, b)
        e = jnp.exp(top_s - top_s[0:1, :])
        off = pl.multiple_of(h * PEER_TOPK, PEER_TOPK)
        idx_sc[pl.ds(off, PEER_TOPK), :] = expert
        g_ref[pl.ds(off, PEER_TOPK), :] = e / jnp.sum(e, axis=0, keepdims=True)

    def body(h, carry):
        nxt = key_stage(h + 1)
        pair_stage(h, *carry)
        return nxt

    last = lax.fori_loop(0, PEER_HEADS - 1, body, key_stage(0))
    pair_stage(PEER_HEADS - 1, *last)
    idx_ref[...] = idx_sc[...].T.astype(jnp.int32)


def _topk(sc, tt):
    t = sc.shape[2]
    return pl.pallas_call(
        _topk_kernel,
        grid=(t // tt,),
        in_specs=[pl.BlockSpec((2 * PEER_HEADS, PEER_N_KEYS, tt), lambda i: (0, 0, i))],
        out_specs=[pl.BlockSpec((tt, PEER_PICKS), lambda i: (i, 0)),
                   pl.BlockSpec((PEER_PICKS, tt), lambda i: (0, i))],
        out_shape=[jax.ShapeDtypeStruct((t, PEER_PICKS), jnp.int32),
                   jax.ShapeDtypeStruct((PEER_PICKS, t), F32)],
        scratch_shapes=[pltpu.VMEM((PEER_PICKS, tt), F32)],
        compiler_params=_params("parallel"),
        name="topk",
    )(sc)


def _gelu(x):
    return 0.5 * x * (1.0 + lax.erf(x * (1.0 / math.sqrt(2.0))))


PEER_SLOTS = 8
PEER_BATCH = 2
PEER_AHEAD = PEER_SLOTS - PEER_BATCH


def _sublane_fold(vregs):
    sub = lax.broadcasted_iota(jnp.int32, (SUBLANES, LANES), 0)
    level, shift = list(vregs), SUBLANES // 2
    while len(level) > 1:
        keep_first = (sub & shift) == 0
        half = len(level) // 2
        level = [jnp.where(keep_first,
                           level[i] + pltpu.roll(level[i], SUBLANES - shift, axis=0),
                           level[i + half] + pltpu.roll(level[i + half], shift, axis=0))
                 for i in range(half)]
        shift //= 2
    return level[0]


def _peer_kernel(idx_ref, g_ref, xn_ref, x1_ref, fw_ref, uv_hbm, o_ref, *scratch):
    bufs, w_sc, sem = scratch[:PEER_SLOTS], scratch[PEER_SLOTS], scratch[PEER_SLOTS + 1]
    tb = xn_ref.shape[0]
    assert tb % PEER_SLOTS == 0 and tb >= 2 * PEER_SLOTS

    def issue(t, slot):
        for j in range(PEER_PICKS):
            pltpu.make_async_copy(uv_hbm.at[idx_ref[t, j]], bufs[slot].at[j], sem.at[slot]).start(priority=j % 2)

    def wait(slot):
        pltpu.make_async_copy(uv_hbm.at[pl.ds(0, PEER_PICKS)], bufs[slot], sem.at[slot]).wait()

    lane = lax.broadcasted_iota(jnp.int32, (PEER_PICKS, tb), 1)

    def compute(t, slot):
        buf = bufs[slot]
        x = xn_ref[t]
        acts = []
        for g in range(PEER_PICKS // SUBLANES):
            prods = [buf[g * SUBLANES + k, :, :LANES] * x for k in range(SUBLANES)]
            acts.append(jnp.sum(_sublane_fold(prods), axis=1, keepdims=True))
        act = jnp.concatenate(acts, axis=0)
        gate = jnp.sum(jnp.where(lane == t, g_ref[...], 0.0), axis=1, keepdims=True)
        w_sc[slot] = jnp.broadcast_to(gate * _gelu(act), (PEER_PICKS, LANES))
        partial = [jnp.zeros((SUBLANES, LANES), F32) for _ in range(4)]
        for j in range(PEER_PICKS):
            partial[j % 4] = partial[j % 4] + (jnp.broadcast_to(w_sc[slot, pl.ds(j, 1), :], (SUBLANES, LANES))
                                               * buf[j, :, LANES:])
        r = x1_ref[t] + ((partial[0] + partial[1]) + (partial[2] + partial[3]))
        ms = jnp.sum(jnp.sum(r * r, axis=1, keepdims=True), axis=0, keepdims=True) * (1.0 / D_MODEL)
        o_ref[t] = r * lax.rsqrt(ms + NORM_EPS) * fw_ref[...]

    def group(base, last):
        for s0 in range(0, PEER_SLOTS, PEER_BATCH):
            batch = range(s0, s0 + PEER_BATCH)
            for s in batch:
                wait(s)
            for s in batch:
                if not last or s + PEER_AHEAD < PEER_SLOTS:
                    issue(base + s + PEER_AHEAD, (s + PEER_AHEAD) % PEER_SLOTS)
            for s in batch:
                compute(base + s, s)

    for s in range(PEER_AHEAD):
        issue(s, s)

    def steady(i, carry):
        group(i * PEER_SLOTS, False)
        return carry

    lax.fori_loop(0, tb // PEER_SLOTS - 1, steady, 0)
    group(tb - PEER_SLOTS, True)


def _as_vregs(a):
    return a.reshape(*a.shape[:-1], SUBLANES, LANES)


def _peer(idx, g, xn, x1, fw, uv, tb):
    t, d = xn.shape
    tok = pl.BlockSpec((tb, SUBLANES, LANES), lambda i: (i, 0, 0))
    out = pl.pallas_call(
        _peer_kernel,
        grid=(t // tb,),
        in_specs=[pl.BlockSpec((tb, PEER_PICKS), lambda i: (i, 0), memory_space=pltpu.SMEM),
                  pl.BlockSpec((PEER_PICKS, tb), lambda i: (0, i)),
                  tok, tok, pl.BlockSpec((SUBLANES, LANES), lambda i: (0, 0)),
                  pl.BlockSpec(memory_space=pl.ANY)],
        out_specs=tok,
        out_shape=jax.ShapeDtypeStruct((t, SUBLANES, LANES), F32),
        scratch_shapes=([pltpu.VMEM((PEER_PICKS, SUBLANES, 2 * LANES), F32) for _ in range(PEER_SLOTS)]
                        + [pltpu.VMEM((PEER_SLOTS, PEER_PICKS, LANES), F32),
                           pltpu.SemaphoreType.DMA((PEER_SLOTS,))]),
        compiler_params=_params("arbitrary"),
        name="peer",
    )(idx, g, _as_vregs(xn), _as_vregs(x1), _as_vregs(fw)[0], uv)
    return out.reshape(t, d)


def _forward(x, p):
    b, s, d = x.shape
    tk = min(512, s // 2)
    ch = 2 * LANES
    qat, ka, vat, qbt, kb, vbt = _prep(x, p["anw"], p["w_fused"], _position_tables(s),
                                       p["qnw"], p["qnwp"], p["knw"], p["knwp"], p["ones"], tk)
    oa = _diff_attn(qat, ka, vat, p["lq1"], p["lk1"], p["lq2"], p["lk2"], p["sw"], min(512, s), tk, ch)
    ob = _gqa_attn(qbt, kb, vbt, min(LANES, s), tk, ch)
    t = b * s
    x1, xn, sc = _mix(oa.reshape(t, A_COLS), ob.reshape(t, B_Q_COLS), x.reshape(t, d),
                      p["wo"], p["fnw"], p["wq"], p["keys"], min(512, t))
    idx, g = _topk(sc, LANES)
    y = _peer(idx, g, xn, x1, p["fw"], p["uv"], min(2 * LANES, t))
    return y.reshape(b, s, d)


def kernel(x_prompt, x_sample, attn_norm, w_in, lambda_q1, lambda_k1, lambda_q2, lambda_k2, subln_w,
           q_norm_w, k_norm_w, w_out, ffn_norm, peer_wq, peer_keys, peer_u, peer_v, final_norm):
    perm_b, _ = _rotate_half_perm(B_Q_COLS, B_DIM // 2)
    qnw = jnp.tile(q_norm_w[0], B_HEADS)
    knw = jnp.tile(k_norm_w[0], B_KV_HEADS)
    p = dict(
        anw=attn_norm[0][None, :],
        w_fused=_fused_in_weight(w_in[0]),
        qnw=qnw[None, :], qnwp=qnw[perm_b][None, :],
        knw=knw[None, :], knwp=knw[perm_b[:B_KV_COLS]][None, :],
        ones=_group_ones(B_Q_COLS, B_DIM),
        lq1=lambda_q1[0][None, :], lk1=lambda_k1[0][None, :],
        lq2=lambda_q2[0][None, :], lk2=lambda_k2[0][None, :],
        sw=subln_w[0][None, :],
        wo=w_out[0].astype(BF16), fnw=ffn_norm[0][None, :],
        wq=peer_wq[0].astype(BF16),
        keys=peer_keys[0].reshape(2 * PEER_HEADS, PEER_N_KEYS, PEER_D_HALF).astype(BF16),
        uv=jnp.concatenate([_as_vregs(peer_u[0]), _as_vregs(peer_v[0])], axis=2), fw=final_norm[None, :],
    )
    return _forward(x_prompt, p), _forward(x_sample, p)
```

```python
import functools
import math

import numpy as np
import jax
import jax.numpy as jnp
from jax import lax
from jax.experimental import pallas as pl
from jax.experimental.pallas import tpu as pltpu

F32 = jnp.float32
BF16 = jnp.bfloat16

D_MODEL = 1024
A_HEADS = 4
A_DIM = 64
A_VDIM = 2 * A_DIM
B_HEADS = 8
B_KV_HEADS = 2
B_DIM = 64
B_GROUP = B_HEADS // B_KV_HEADS
A_COLS = A_HEADS * 2 * A_DIM
B_Q_COLS = B_HEADS * B_DIM
B_KV_COLS = B_KV_HEADS * B_DIM
ROPE_THETA = 10000.0
GRID_W = 64
NORM_EPS = 1e-6
LAMBDA_INIT = 0.8 - 0.6 * math.exp(-0.3)
PEER_HEADS = 8
PEER_N_KEYS = 128
PEER_D_HALF = 128
PEER_TOPK = 16
PEER_PICKS = PEER_HEADS * PEER_TOPK

LANES = 128
SUBLANES = 8
HALF = LANES // 2
BF16_SUBLANES = 16
ONES_ROWS = BF16_SUBLANES
V_ROWS_A = A_VDIM + ONES_ROWS
V_ROWS_B = B_DIM + ONES_ROWS
VMEM_LIMIT_BYTES = 56 * 1024 * 1024

OFF_QA, OFF_QA_ROT = 0, 512
OFF_KA, OFF_KA_ROT = 1024, 1536
OFF_VA = 2048
OFF_QB, OFF_QB_ROT = 2560, 3072
OFF_KB, OFF_KB_ROT = 3584, 3712
OFF_VB = 3840
FUSED_COLS = 3968


def _params(*semantics):
    return pltpu.CompilerParams(dimension_semantics=semantics, vmem_limit_bytes=VMEM_LIMIT_BYTES)


def _rope_tables(pos, dim):
    inv = ROPE_THETA ** (-jnp.arange(0, dim, 2, dtype=F32) / dim)
    ang = pos[:, None] * inv[None, :]
    ang = jnp.concatenate([ang, ang], axis=-1)
    return jnp.cos(ang), jnp.sin(ang)


def _position_tables(s):
    rows = s // GRID_W
    t = jnp.arange(s, dtype=F32)
    row = jnp.broadcast_to(jnp.arange(rows, dtype=F32)[:, None], (rows, GRID_W)).reshape(s)
    col = jnp.broadcast_to(jnp.arange(GRID_W, dtype=F32)[None, :], (rows, GRID_W)).reshape(s)
    cos_a, sin_a = _rope_tables(t, A_DIM)
    cos_r, sin_r = _rope_tables(row, B_DIM // 2)
    cos_c, sin_c = _rope_tables(col, B_DIM // 2)
    cos_b = jnp.concatenate([cos_r, cos_c], axis=-1)
    sin_b = jnp.concatenate([sin_r, sin_c], axis=-1)
    two = lambda a: jnp.concatenate([a, a], axis=-1)
    return two(cos_a), two(sin_a), two(cos_b), two(sin_b)


def _rotate_half_perm(n_cols, block):
    i = np.arange(n_cols)
    o = i % block
    half = block // 2
    perm = np.where(o < half, i + half, i - half)
    sign = np.where(o < half, -1.0, 1.0).astype(np.float32)
    return perm, sign


def _fused_in_weight(w_in):
    wqa = w_in[:, 0:512]
    wka = w_in[:, 512:1024]
    wva = w_in[:, 1024:1536]
    wqb = w_in[:, 1536:2048]
    wkb = w_in[:, 2048:2176]
    wvb = w_in[:, 2176:2304]

    def rot(m, block):
        perm, sign = _rotate_half_perm(m.shape[1], block)
        return m[:, perm] * sign

    parts = [wqa, rot(wqa, A_DIM), wka, rot(wka, A_DIM), wva,
             wqb, rot(wqb, B_DIM // 2), wkb, rot(wkb, B_DIM // 2), wvb]
    return jnp.concatenate(parts, axis=1).astype(BF16)


def _group_ones(n, group):
    i = np.arange(n) // group
    return jnp.asarray((i[:, None] == i[None, :]).astype(np.float32), dtype=BF16)


def _rms(x):
    return x * lax.rsqrt(jnp.mean(x * x, axis=-1, keepdims=True) + NORM_EPS)


def _group_mean_sq(p, ones):
    sq = p * p
    hi = sq.astype(BF16)
    lo = (sq - hi.astype(F32)).astype(BF16)
    ss = jnp.dot(hi, ones, preferred_element_type=F32) + jnp.dot(lo, ones, preferred_element_type=F32)
    return ss * (1.0 / B_DIM)


def _prep_kernel(x_ref, anw_ref, w_ref, cosa_ref, sina_ref, cosb_ref, sinb_ref,
                 qnw_ref, qnwp_ref, knw_ref, knwp_ref, ones_ref,
                 qat_ref, ka_ref, vat_ref, qbt_ref, kb_ref, vbt_ref):
    x = x_ref[0]
    tm = x.shape[0]
    hb = (_rms(x) * anw_ref[...]).astype(BF16)

    def proj(off, n):
        return jnp.dot(hb, w_ref[:, off:off + n], preferred_element_type=F32)

    def tile_lanes(a, n):
        return jnp.concatenate([a] * n, axis=1)

    zeros_half = jnp.zeros((HALF, tm), F32)
    ones_rows = jnp.ones((ONES_ROWS, tm), BF16)

    cosa = tile_lanes(cosa_ref[...], A_COLS // LANES)
    sina = tile_lanes(sina_ref[...], A_COLS // LANES)
    qa = (proj(OFF_QA, A_COLS) * cosa + proj(OFF_QA_ROT, A_COLS) * sina) * (A_DIM ** -0.5)
    ka = proj(OFF_KA, A_COLS) * cosa + proj(OFF_KA_ROT, A_COLS) * sina
    va = proj(OFF_VA, A_COLS)
    for h in range(A_HEADS):
        qt = qa[:, h * LANES:(h + 1) * LANES].T
        qat_ref[0, 2 * h] = jnp.concatenate([qt[:HALF], zeros_half], axis=0).astype(BF16)
        qat_ref[0, 2 * h + 1] = jnp.concatenate([zeros_half, qt[HALF:]], axis=0).astype(BF16)
        vat_ref[0, h, 0, :LANES, :] = va[:, h * LANES:(h + 1) * LANES].T.astype(BF16)
        vat_ref[0, h, 0, LANES:, :] = ones_rows
    ka_ref[0] = ka.astype(BF16)

    cosb = tile_lanes(cosb_ref[...], B_Q_COLS // LANES)
    sinb = tile_lanes(sinb_ref[...], B_Q_COLS // LANES)
    pq = proj(OFF_QB, B_Q_COLS)
    rq = lax.rsqrt(_group_mean_sq(pq, ones_ref[...]) + NORM_EPS)
    qn = pq * rq * qnw_ref[...]
    qn_rot = proj(OFF_QB_ROT, B_Q_COLS) * rq * qnwp_ref[...]
    qb = (qn * cosb + qn_rot * sinb) * (B_DIM ** -0.5)
    for c in range(B_HEADS // 2):
        qt = qb[:, c * LANES:(c + 1) * LANES].T
        for parity in range(2):
            h = 2 * c + parity
            piece = qt[parity * HALF:(parity + 1) * HALF]
            parts = [piece, zeros_half] if h // B_GROUP == 0 else [zeros_half, piece]
            qbt_ref[0, h] = jnp.concatenate(parts, axis=0).astype(BF16)

    assert OFF_KB_ROT == OFF_KB + B_KV_COLS
    pk2 = proj(OFF_KB, 2 * B_KV_COLS)
    pk = pk2[:, :B_KV_COLS]
    rk = lax.rsqrt(_group_mean_sq(pk, ones_ref[:B_KV_COLS, :B_KV_COLS]) + NORM_EPS)
    kn = pk * rk * knw_ref[...]
    kn_rot = pk2[:, B_KV_COLS:] * rk * knwp_ref[...]
    kb_ref[0] = (kn * cosb_ref[...] + kn_rot * sinb_ref[...]).astype(BF16)
    vbt = proj(OFF_VB, B_KV_COLS).T.astype(BF16)
    for g in range(B_KV_HEADS):
        vbt_ref[0, 0, g, :B_DIM, :] = vbt[g * B_DIM:(g + 1) * B_DIM]
        vbt_ref[0, 0, g, B_DIM:, :] = ones_rows


def _prep(x, anw, w_fused, tables, qnw, qnwp, knw, knwp, ones, tm):
    b, s, d = x.shape
    nt = s // tm
    cosa, sina, cosb, sinb = tables
    full = lambda shape: pl.BlockSpec(shape, lambda bi, i: (0,) * len(shape))
    tab = pl.BlockSpec((tm, LANES), lambda bi, i: (i, 0))
    tok = lambda n: pl.BlockSpec((1, tm, n), lambda bi, i: (bi, i, 0))
    qt = lambda heads: pl.BlockSpec((1, heads, LANES, tm), lambda bi, i: (bi, 0, 0, i))
    sds = lambda *shape: jax.ShapeDtypeStruct(shape, BF16)
    return pl.pallas_call(
        _prep_kernel,
        grid=(b, nt),
        in_specs=[pl.BlockSpec((1, tm, d), lambda bi, i: (bi, i, 0)),
                  full((1, d)), full((d, FUSED_COLS)), tab, tab, tab, tab,
                  full((1, B_Q_COLS)), full((1, B_Q_COLS)), full((1, B_KV_COLS)), full((1, B_KV_COLS)),
                  full((B_Q_COLS, B_Q_COLS))],
        out_specs=[qt(2 * A_HEADS), tok(A_COLS),
                   pl.BlockSpec((1, A_HEADS, 1, V_ROWS_A, tm), lambda bi, i: (bi, 0, i, 0, 0)),
                   qt(B_HEADS), tok(B_KV_COLS),
                   pl.BlockSpec((1, 1, B_KV_HEADS, V_ROWS_B, tm), lambda bi, i: (bi, i, 0, 0, 0))],
        out_shape=[sds(b, 2 * A_HEADS, LANES, s), sds(b, s, A_COLS), sds(b, A_HEADS, nt, V_ROWS_A, tm),
                   sds(b, B_HEADS, LANES, s), sds(b, s, B_KV_COLS), sds(b, nt, B_KV_HEADS, V_ROWS_B, tm)],
        compiler_params=_params("parallel", "parallel"),
        name="prep",
    )(x, anw, w_fused, cosa, sina, cosb, sinb, qnw, qnwp, knw, knwp, ones)


def _flash_scratch(rows, tk, v_rows):
    return [pltpu.VMEM((1, rows), F32), pltpu.VMEM((v_rows, rows), F32),
            pltpu.VMEM((tk, rows), F32), pltpu.VMEM((tk, rows), F32)]


def _flash_t(q_chunk, n_chunks, ch, k_ref, vt_tile, m_sc, acc_sc, s0_sc, s1_sc):
    m_sc[...] = jnp.full(m_sc.shape, -jnp.inf, F32)
    acc_sc[...] = jnp.zeros(acc_sc.shape, F32)
    tk = s0_sc.shape[0]
    nk = k_ref.shape[1] // tk
    assert nk >= 2 and nk % 2 == 0, "key tiles are processed in pairs"
    chunks = [slice(c * ch, (c + 1) * ch) for c in range(n_chunks)]

    def scores(j, c, dst):
        k = k_ref[0, pl.ds(pl.multiple_of(j * tk, tk), tk), :]
        dst[:, chunks[c]] = jnp.dot(k, q_chunk(c), preferred_element_type=F32)

    def softmax_pv(j, c, src):
        sl = chunks[c]
        s = src[:, sl]
        m_prev = m_sc[:, sl]
        m_new = jnp.maximum(m_prev, jnp.max(s, axis=0, keepdims=True))
        p = jnp.exp(s - m_new).astype(BF16)
        acc_sc[:, sl] = (jnp.exp(m_prev - m_new) * acc_sc[:, sl]
                         + jnp.dot(vt_tile(j, c), p, preferred_element_type=F32))
        m_sc[:, sl] = m_new

    def step(j, src, dst):
        for c in range(n_chunks):
            if dst is not None:
                scores(j + 1, c, dst)
            softmax_pv(j, c, src)

    for c in range(n_chunks):
        scores(0, c, s0_sc)

    def body(i, carry):
        step(2 * i, s0_sc, s1_sc)
        step(2 * i + 1, s1_sc, s0_sc)
        return carry

    lax.fori_loop(0, nk // 2 - 1, body, 0)
    step(nk - 2, s0_sc, s1_sc)
    step(nk - 1, s1_sc, None)
    acc = acc_sc[...]
    n_val = acc.shape[0] - ONES_ROWS
    return acc[:n_val] / acc[n_val:n_val + 1]


def _diff_attn_kernel(qt_ref, k_ref, vt_ref, lq1_ref, lk1_ref, lq2_ref, lk2_ref, sw_ref,
                      o_ref, *scratch, ch):
    tq = qt_ref.shape[3]
    per_map = tq // ch

    def q_chunk(c):
        return qt_ref[0, c // per_map, :, (c % per_map) * ch:(c % per_map + 1) * ch]

    ot = _flash_t(q_chunk, 2 * per_map, ch, k_ref, lambda j, c: vt_ref[0, 0, j], *scratch)
    lam = (jnp.exp(jnp.sum(lq1_ref[...] * lk1_ref[...], axis=1, keepdims=True))
           - jnp.exp(jnp.sum(lq2_ref[...] * lk2_ref[...], axis=1, keepdims=True)) + LAMBDA_INIT)
    d = (ot[:, :tq] - lam * ot[:, tq:]).T
    o_ref[0] = (_rms(d) * sw_ref[...] * (1.0 - LAMBDA_INIT)).astype(BF16)


def _diff_attn(qat, ka, vat, lq1, lk1, lq2, lk2, sw, tq, tk, ch):
    b, s, _ = ka.shape
    small = pl.BlockSpec((1, A_DIM), lambda bi, h, i: (0, 0))
    return pl.pallas_call(
        functools.partial(_diff_attn_kernel, ch=ch),
        grid=(b, A_HEADS, s // tq),
        in_specs=[pl.BlockSpec((1, 2, LANES, tq), lambda bi, h, i: (bi, h, 0, i)),
                  pl.BlockSpec((1, s, LANES), lambda bi, h, i: (bi, 0, h)),
                  pl.BlockSpec((1, 1, s // tk, V_ROWS_A, tk), lambda bi, h, i: (bi, h, 0, 0, 0)),
                  small, small, small, small, pl.BlockSpec((1, A_VDIM), lambda bi, h, i: (0, 0))],
        out_specs=pl.BlockSpec((1, tq, LANES), lambda bi, h, i: (bi, i, h)),
        out_shape=jax.ShapeDtypeStruct((b, s, A_COLS), BF16),
        scratch_shapes=_flash_scratch(2 * tq, tk, V_ROWS_A),
        compiler_params=_params("parallel", "parallel", "arbitrary"),
        name="diff_attn",
    )(qat, ka, vat, lq1, lk1, lq2, lk2, sw)


def _gqa_kernel(qt_ref, k_ref, vt_ref, o_ref, *scratch, ch):
    tq = qt_ref.shape[3]
    n_chunks = B_HEADS * tq // ch
    first_head = lambda c: c * ch // tq

    def q_chunk(c):
        if tq >= ch:
            off = c * ch % tq
            return qt_ref[0, first_head(c), :, off:off + ch]
        return jnp.concatenate([qt_ref[0, first_head(c) + i] for i in range(ch // tq)], axis=1)

    def vt_tile(j, c):
        return vt_ref[0, j, first_head(c) // B_GROUP]

    ot = _flash_t(q_chunk, n_chunks, ch, k_ref, vt_tile, *scratch)
    pairs = [ot[:, (2 * c) * tq:(2 * c + 2) * tq] for c in range(B_HEADS // 2)]
    o_ref[0] = jnp.concatenate(
        [jnp.concatenate([p[:, :tq], p[:, tq:]], axis=0).T for p in pairs], axis=1).astype(BF16)


def _gqa_attn(qbt, kb, vbt, tq, tk, ch):
    b, s, _ = kb.shape
    return pl.pallas_call(
        functools.partial(_gqa_kernel, ch=ch),
        grid=(b, s // tq),
        in_specs=[pl.BlockSpec((1, B_HEADS, LANES, tq), lambda bi, i: (bi, 0, 0, i)),
                  pl.BlockSpec((1, s, LANES), lambda bi, i: (bi, 0, 0)),
                  pl.BlockSpec((1, s // tk, B_KV_HEADS, V_ROWS_B, tk), lambda bi, i: (bi, 0, 0, 0, 0))],
        out_specs=pl.BlockSpec((1, tq, B_Q_COLS), lambda bi, i: (bi, i, 0)),
        out_shape=jax.ShapeDtypeStruct((b, s, B_Q_COLS), BF16),
        scratch_shapes=_flash_scratch(B_HEADS * tq, tk, V_ROWS_B),
        compiler_params=_params("parallel", "arbitrary"),
        name="gqa_attn",
    )(qbt, kb, vbt)


def _mix_kernel(oa_ref, ob_ref, x_ref, wo_ref, fnw_ref, wq_ref, keys_ref, x1_ref, xn_ref, sc_ref):
    y = (jnp.dot(oa_ref[...], wo_ref[:A_COLS, :], preferred_element_type=F32)
         + jnp.dot(ob_ref[...], wo_ref[A_COLS:, :], preferred_element_type=F32))
    x1 = x_ref[...] + y
    xn = _rms(x1) * fnw_ref[...]
    x1_ref[...] = x1
    xn_ref[...] = xn
    xb = xn.astype(BF16)
    for h in range(PEER_HEADS):
        q = jnp.dot(xb, wq_ref[:, h * 2 * PEER_D_HALF:(h + 1) * 2 * PEER_D_HALF],
                    preferred_element_type=F32).astype(BF16)
        for c in range(2):
            sc_ref[2 * h + c] = lax.dot_general(
                keys_ref[2 * h + c], q[:, c * PEER_D_HALF:(c + 1) * PEER_D_HALF], (((1,), (1,)), ((), ())),
                preferred_element_type=F32)


def _mix(oa, ob, x, wo, fnw, wq, keys, tm):
    t, d = x.shape
    nq = wq.shape[1]
    row = lambda n: pl.BlockSpec((tm, n), lambda i: (i, 0))
    full = lambda shape: pl.BlockSpec(shape, lambda i: (0,) * len(shape))
    return pl.pallas_call(
        _mix_kernel,
        grid=(t // tm,),
        in_specs=[row(A_COLS), row(B_Q_COLS), row(d), full((d, d)), full((1, d)), full((d, nq)),
                  full((2 * PEER_HEADS, PEER_N_KEYS, PEER_D_HALF))],
        out_specs=[row(d), row(d),
                   pl.BlockSpec((2 * PEER_HEADS, PEER_N_KEYS, tm), lambda i: (0, 0, i))],
        out_shape=[jax.ShapeDtypeStruct((t, d), F32), jax.ShapeDtypeStruct((t, d), F32),
                   jax.ShapeDtypeStruct((2 * PEER_HEADS, PEER_N_KEYS, t), F32)],
        compiler_params=_params("parallel"),
        name="mix",
    )(oa, ob, x, wo, fnw, wq, keys)


_NO_ROW = float(2 ** 20)


def _topk_rows(x, iota, k):
    vals, rows = [], []
    for _ in range(k):
        m = jnp.max(x, axis=0, keepdims=True)
        r = jnp.min(jnp.where(x == m, iota, _NO_ROW), axis=0, keepdims=True)
        vals.append(m)
        rows.append(r)
        x = jnp.where(iota == r, -jnp.inf, x)
    return jnp.concatenate(vals, axis=0), jnp.concatenate(rows, axis=0)


def _select_rows(table, sel):
    out = jnp.zeros_like(sel)
    for r in range(table.shape[0]):
        out = jnp.where(sel == float(r), table[r:r + 1, :], out)
    return out


def _pair_candidates(s1, s2, sub):
    lo, hi = s2[:SUBLANES], s2[SUBLANES:]
    vals, ids = [s1[0:1] + lo, s1[0:1] + hi], [sub, sub + float(SUBLANES)]
    for a in range(1, SUBLANES):
        bound = PEER_TOPK // (a + 1)
        v = s1[a:a + 1] + lo
        vals.append(v if bound >= SUBLANES else jnp.where(sub < float(bound), v, -jnp.inf))
        ids.append(sub + float(a * PEER_TOPK))
    vals.append(s1[SUBLANES:] + s2[0:1])
    ids.append((sub + float(SUBLANES)) * float(PEER_TOPK))
    return jnp.concatenate(vals, axis=0), jnp.concatenate(ids, axis=0)


def _topk_kernel(sc_ref, idx_ref, g_ref, idx_sc):
    tt = sc_ref.shape[2]
    iota_keys = lax.broadcasted_iota(jnp.int32, (PEER_N_KEYS, tt), 0).astype(F32)
    sub = lax.broadcasted_iota(jnp.int32, (SUBLANES, tt), 0).astype(F32)

    def key_stage(h):
        s1, i1 = _topk_rows(sc_ref[2 * h], iota_keys, PEER_TOPK)
        s2, i2 = _topk_rows(sc_ref[2 * h + 1], iota_keys, PEER_TOPK)
        return s1, i1, s2, i2

    def pair_stage(h, s1, i1, s2, i2):
        top_s, pos = _topk_rows(*_pair_candidates(s1, s2, sub), PEER_TOPK)
        a = jnp.floor(pos * (1.0 / PEER_TOPK))
        b = pos - a * PEER_TOPK
        expert = _select_rows(i1, a) * PEER_N_KEYS + _select_rows(i2, b)
        e = jnp.exp(top_s - top_s[0:1, :])
        off = pl.multiple_of(h * PEER_TOPK, PEER_TOPK)
        idx_sc[pl.ds(off, PEER_TOPK), :] = expert
        g_ref[pl.ds(off, PEER_TOPK), :] = e / jnp.sum(e, axis=0, keepdims=True)

    def body(h, carry):
        nxt = key_stage(h + 1)
        pair_stage(h, *carry)
        return nxt

    last = lax.fori_loop(0, PEER_HEADS - 1, body, key_stage(0))
    pair_stage(PEER_HEADS - 1, *last)
    idx_ref[...] = idx_sc[...].T.astype(jnp.int32)


def _topk(sc, tt):
    t = sc.shape[2]
    return pl.pallas_call(
        _topk_kernel,
        grid=(t // tt,),
        in_specs=[pl.BlockSpec((2 * PEER_HEADS, PEER_N_KEYS, tt), lambda i: (0, 0, i))],
        out_specs=[pl.BlockSpec((tt, PEER_PICKS), lambda i: (i, 0)),
                   pl.BlockSpec((PEER_PICKS, tt), lambda i: (0, i))],
        out_shape=[jax.ShapeDtypeStruct((t, PEER_PICKS), jnp.int32),
                   jax.ShapeDtypeStruct((PEER_PICKS, t), F32)],
        scratch_shapes=[pltpu.VMEM((PEER_PICKS, tt), F32)],
        compiler_params=_params("parallel"),
        name="topk",
    )(sc)


def _gelu(x):
    return 0.5 * x * (1.0 + lax.erf(x * (1.0 / math.sqrt(2.0))))


PEER_SLOTS = 8
PEER_BATCH = 2
PEER_AHEAD = PEER_SLOTS - PEER_BATCH


def _sublane_fold(vregs):
    sub = lax.broadcasted_iota(jnp.int32, (SUBLANES, LANES), 0)
    level, shift = list(vregs), SUBLANES // 2
    while len(level) > 1:
        keep_first = (sub & shift) == 0
        half = len(level) // 2
        level = [jnp.where(keep_first,
                           level[i] + pltpu.roll(level[i], SUBLANES - shift, axis=0),
                           level[i + half] + pltpu.roll(level[i + half], shift, axis=0))
                 for i in range(half)]
        shift //= 2
    return level[0]


def _peer_kernel(idx_ref, g_ref, xn_ref, x1_ref, fw_ref, uv_hbm, o_ref, *scratch):
    bufs, w_sc, sem = scratch[:PEER_SLOTS], scratch[PEER_SLOTS], scratch[PEER_SLOTS + 1]
    tb = xn_ref.shape[0]
    assert tb % PEER_SLOTS == 0 and tb >= 2 * PEER_SLOTS

    def issue(t, slot):
        for j in range(PEER_PICKS):
            pltpu.make_async_copy(uv_hbm.at[idx_ref[t, j]], bufs[slot].at[j], sem.at[slot]).start(priority=j % 2)

    def wait(slot):
        pltpu.make_async_copy(uv_hbm.at[pl.ds(0, PEER_PICKS)], bufs[slot], sem.at[slot]).wait()

    lane = lax.broadcasted_iota(jnp.int32, (PEER_PICKS, tb), 1)

    def compute(t, slot):
        buf = bufs[slot]
        x = xn_ref[t]
        acts = []
        for g in range(PEER_PICKS // SUBLANES):
            prods = [buf[g * SUBLANES + k, :, :LANES] * x for k in range(SUBLANES)]
            acts.append(jnp.sum(_sublane_fold(prods), axis=1, keepdims=True))
        act = jnp.concatenate(acts, axis=0)
        gate = jnp.sum(jnp.where(lane == t, g_ref[...], 0.0), axis=1, keepdims=True)
        w_sc[slot] = jnp.broadcast_to(gate * _gelu(act), (PEER_PICKS, LANES))
        partial = [jnp.zeros((SUBLANES, LANES), F32) for _ in range(4)]
        for j in range(PEER_PICKS):
            partial[j % 4] = partial[j % 4] + (jnp.broadcast_to(w_sc[slot, pl.ds(j, 1), :], (SUBLANES, LANES))
                                               * buf[j, :, LANES:])
        r = x1_ref[t] + ((partial[0] + partial[1]) + (partial[2] + partial[3]))
        ms = jnp.sum(jnp.sum(r * r, axis=1, keepdims=True), axis=0, keepdims=True) * (1.0 / D_MODEL)
        o_ref[t] = r * lax.rsqrt(ms + NORM_EPS) * fw_ref[...]

    def group(base, last):
        for s0 in range(0, PEER_SLOTS, PEER_BATCH):
            batch = range(s0, s0 + PEER_BATCH)
            for s in batch:
                wait(s)
            for s in batch:
                if not last or s + PEER_AHEAD < PEER_SLOTS:
                    issue(base + s + PEER_AHEAD, (s + PEER_AHEAD) % PEER_SLOTS)
            for s in batch:
                compute(base + s, s)

    for s in range(PEER_AHEAD):
        issue(s, s)

    def steady(i, carry):
        group(i * PEER_SLOTS, False)
        return carry

    lax.fori_loop(0, tb // PEER_SLOTS - 1, steady, 0)
    group(tb - PEER_SLOTS, True)


def _as_vregs(a):
    return a.reshape(*a.shape[:-1], SUBLANES, LANES)


def _peer(idx, g, xn, x1, fw, uv, tb):
    t, d = xn.shape
    tok = pl.BlockSpec((tb, SUBLANES, LANES), lambda i: (i, 0, 0))
    out = pl.pallas_call(
        _peer_kernel,
        grid=(t // tb,),
        in_specs=[pl.BlockSpec((tb, PEER_PICKS), lambda i: (i, 0), memory_space=pltpu.SMEM),
                  pl.BlockSpec((PEER_PICKS, tb), lambda i: (0, i)),
                  tok, tok, pl.BlockSpec((SUBLANES, LANES), lambda i: (0, 0)),
                  pl.BlockSpec(memory_space=pl.ANY)],
        out_specs=tok,
        out_shape=jax.ShapeDtypeStruct((t, SUBLANES, LANES), F32),
        scratch_shapes=([pltpu.VMEM((PEER_PICKS, SUBLANES, 2 * LANES), F32) for _ in range(PEER_SLOTS)]
                        + [pltpu.VMEM((PEER_SLOTS, PEER_PICKS, LANES), F32),
                           pltpu.SemaphoreType.DMA((PEER_SLOTS,))]),
        compiler_params=_params("arbitrary"),
        name="peer",
    )(idx, g, _as_vregs(xn), _as_vregs(x1), _as_vregs(fw)[0], uv)
    return out.reshape(t, d)


def _forward(x, p):
    b, s, d = x.shape
    tk = min(512, s // 2)
    ch = 2 * LANES
    qat, ka, vat, qbt, kb, vbt = _prep(x, p["anw"], p["w_fused"], _position_tables(s),
                                       p["qnw"], p["qnwp"], p["knw"], p["knwp"], p["ones"], tk)
    oa = _diff_attn(qat, ka, vat, p["lq1"], p["lk1"], p["lq2"], p["lk2"], p["sw"], min(2048, s), tk, ch)
    ob = _gqa_attn(qbt, kb, vbt, min(4 * LANES, s), tk, ch)
    t = b * s
    x1, xn, sc = _mix(oa.reshape(t, A_COLS), ob.reshape(t, B_Q_COLS), x.reshape(t, d),
                      p["wo"], p["fnw"], p["wq"], p["keys"], min(512, t))
    idx, g = _topk(sc, LANES)
    y = _peer(idx, g, xn, x1, p["fw"], p["uv"], min(2 * LANES, t))
    return y.reshape(b, s, d)


def kernel(x_prompt, x_sample, attn_norm, w_in, lambda_q1, lambda_k1, lambda_q2, lambda_k2, subln_w,
           q_norm_w, k_norm_w, w_out, ffn_norm, peer_wq, peer_keys, peer_u, peer_v, final_norm):
    perm_b, _ = _rotate_half_perm(B_Q_COLS, B_DIM // 2)
    qnw = jnp.tile(q_norm_w[0], B_HEADS)
    knw = jnp.tile(k_norm_w[0], B_KV_HEADS)
    p = dict(
        anw=attn_norm[0][None, :],
        w_fused=_fused_in_weight(w_in[0]),
        qnw=qnw[None, :], qnwp=qnw[perm_b][None, :],
        knw=knw[None, :], knwp=knw[perm_b[:B_KV_COLS]][None, :],
        ones=_group_ones(B_Q_COLS, B_DIM),
        lq1=lambda_q1[0][None, :], lk1=lambda_k1[0][None, :],
        lq2=lambda_q2[0][None, :], lk2=lambda_k2[0][None, :],
        sw=subln_w[0][None, :],
        wo=w_out[0].astype(BF16), fnw=ffn_norm[0][None, :],
        wq=peer_wq[0].astype(BF16),
        keys=peer_keys[0].reshape(2 * PEER_HEADS, PEER_N_KEYS, PEER_D_HALF).astype(BF16),
        uv=jnp.concatenate([_as_vregs(peer_u[0]), _as_vregs(peer_v[0])], axis=2), fw=final_norm[None, :],
    )
    return _forward(x_prompt, p), _forward(x_sample, p)
```

```python
import functools
import math

import numpy as np
import jax
import jax.numpy as jnp
from jax import lax
from jax.experimental import pallas as pl
from jax.experimental.pallas import tpu as pltpu

F32 = jnp.float32
BF16 = jnp.bfloat16

D_MODEL = 1024
A_HEADS = 4
A_DIM = 64
A_VDIM = 2 * A_DIM
B_HEADS = 8
B_KV_HEADS = 2
B_DIM = 64
B_GROUP = B_HEADS // B_KV_HEADS
A_COLS = A_HEADS * 2 * A_DIM
B_Q_COLS = B_HEADS * B_DIM
B_KV_COLS = B_KV_HEADS * B_DIM
ROPE_THETA = 10000.0
GRID_W = 64
NORM_EPS = 1e-6
LAMBDA_INIT = 0.8 - 0.6 * math.exp(-0.3)
PEER_HEADS = 8
PEER_N_KEYS = 128
PEER_D_HALF = 128
PEER_TOPK = 16
PEER_PICKS = PEER_HEADS * PEER_TOPK

LANES = 128
SUBLANES = 8
HALF = LANES // 2
BF16_SUBLANES = 16
ONES_ROWS = BF16_SUBLANES
V_ROWS_A = A_VDIM + ONES_ROWS
V_ROWS_B = B_DIM + ONES_ROWS
VMEM_LIMIT_BYTES = 56 * 1024 * 1024

OFF_QA, OFF_QA_ROT = 0, 512
OFF_KA, OFF_KA_ROT = 1024, 1536
OFF_VA = 2048
OFF_QB, OFF_QB_ROT = 2560, 3072
OFF_KB, OFF_KB_ROT = 3584, 3712
OFF_VB = 3840
FUSED_COLS = 3968


def _params(*semantics):
    return pltpu.CompilerParams(dimension_semantics=semantics, vmem_limit_bytes=VMEM_LIMIT_BYTES)


def _rope_tables(pos, dim):
    inv = ROPE_THETA ** (-jnp.arange(0, dim, 2, dtype=F32) / dim)
    ang = pos[:, None] * inv[None, :]
    ang = jnp.concatenate([ang, ang], axis=-1)
    return jnp.cos(ang), jnp.sin(ang)


def _position_tables(s):
    rows = s // GRID_W
    t = jnp.arange(s, dtype=F32)
    row = jnp.broadcast_to(jnp.arange(rows, dtype=F32)[:, None], (rows, GRID_W)).reshape(s)
    col = jnp.broadcast_to(jnp.arange(GRID_W, dtype=F32)[None, :], (rows, GRID_W)).reshape(s)
    cos_a, sin_a = _rope_tables(t, A_DIM)
    cos_r, sin_r = _rope_tables(row, B_DIM // 2)
    cos_c, sin_c = _rope_tables(col, B_DIM // 2)
    cos_b = jnp.concatenate([cos_r, cos_c], axis=-1)
    sin_b = jnp.concatenate([sin_r, sin_c], axis=-1)
    two = lambda a: jnp.concatenate([a, a], axis=-1)
    return two(cos_a), two(sin_a), two(cos_b), two(sin_b)


def _rotate_half_perm(n_cols, block):
    i = np.arange(n_cols)
    o = i % block
    half = block // 2
    perm = np.where(o < half, i + half, i - half)
    sign = np.where(o < half, -1.0, 1.0).astype(np.float32)
    return perm, sign


def _fused_in_weight(w_in):
    wqa = w_in[:, 0:512]
    wka = w_in[:, 512:1024]
    wva = w_in[:, 1024:1536]
    wqb = w_in[:, 1536:2048]
    wkb = w_in[:, 2048:2176]
    wvb = w_in[:, 2176:2304]

    def rot(m, block):
        perm, sign = _rotate_half_perm(m.shape[1], block)
        return m[:, perm] * sign

    parts = [wqa, rot(wqa, A_DIM), wka, rot(wka, A_DIM), wva,
             wqb, rot(wqb, B_DIM // 2), wkb, rot(wkb, B_DIM // 2), wvb]
    return jnp.concatenate(parts, axis=1).astype(BF16)


def _group_ones(n, group):
    i = np.arange(n) // group
    return jnp.asarray((i[:, None] == i[None, :]).astype(np.float32), dtype=BF16)


def _rms(x):
    return x * lax.rsqrt(jnp.mean(x * x, axis=-1, keepdims=True) + NORM_EPS)


def _group_mean_sq(p, ones):
    sq = p * p
    hi = sq.astype(BF16)
    lo = (sq - hi.astype(F32)).astype(BF16)
    ss = jnp.dot(hi, ones, preferred_element_type=F32) + jnp.dot(lo, ones, preferred_element_type=F32)
    return ss * (1.0 / B_DIM)


def _prep_kernel(x_ref, anw_ref, w_ref, cosa_ref, sina_ref, cosb_ref, sinb_ref,
                 qnw_ref, qnwp_ref, knw_ref, knwp_ref, ones_ref,
                 qat_ref, ka_ref, vat_ref, qbt_ref, kb_ref, vbt_ref):
    x = x_ref[0]
    tm = x.shape[0]
    hb = (_rms(x) * anw_ref[...]).astype(BF16)

    def proj(off, n):
        return jnp.dot(hb, w_ref[:, off:off + n], preferred_element_type=F32)

    def tile_lanes(a, n):
        return jnp.concatenate([a] * n, axis=1)

    zeros_half = jnp.zeros((HALF, tm), F32)
    ones_rows = jnp.ones((ONES_ROWS, tm), BF16)

    cosa = tile_lanes(cosa_ref[...], A_COLS // LANES)
    sina = tile_lanes(sina_ref[...], A_COLS // LANES)
    qa = (proj(OFF_QA, A_COLS) * cosa + proj(OFF_QA_ROT, A_COLS) * sina) * (A_DIM ** -0.5)
    ka = proj(OFF_KA, A_COLS) * cosa + proj(OFF_KA_ROT, A_COLS) * sina
    va = proj(OFF_VA, A_COLS)
    for h in range(A_HEADS):
        qt = qa[:, h * LANES:(h + 1) * LANES].T
        qat_ref[0, 2 * h] = jnp.concatenate([qt[:HALF], zeros_half], axis=0).astype(BF16)
        qat_ref[0, 2 * h + 1] = jnp.concatenate([zeros_half, qt[HALF:]], axis=0).astype(BF16)
        vat_ref[0, h, 0, :LANES, :] = va[:, h * LANES:(h + 1) * LANES].T.astype(BF16)
        vat_ref[0, h, 0, LANES:, :] = ones_rows
    ka_ref[0] = ka.astype(BF16)

    cosb = tile_lanes(cosb_ref[...], B_Q_COLS // LANES)
    sinb = tile_lanes(sinb_ref[...], B_Q_COLS // LANES)
    pq = proj(OFF_QB, B_Q_COLS)
    rq = lax.rsqrt(_group_mean_sq(pq, ones_ref[...]) + NORM_EPS)
    qn = pq * rq * qnw_ref[...]
    qn_rot = proj(OFF_QB_ROT, B_Q_COLS) * rq * qnwp_ref[...]
    qb = (qn * cosb + qn_rot * sinb) * (B_DIM ** -0.5)
    for c in range(B_HEADS // 2):
        qt = qb[:, c * LANES:(c + 1) * LANES].T
        for parity in range(2):
            h = 2 * c + parity
            piece = qt[parity * HALF:(parity + 1) * HALF]
            parts = [piece, zeros_half] if h // B_GROUP == 0 else [zeros_half, piece]
            qbt_ref[0, h] = jnp.concatenate(parts, axis=0).astype(BF16)

    assert OFF_KB_ROT == OFF_KB + B_KV_COLS
    pk2 = proj(OFF_KB, 2 * B_KV_COLS)
    pk = pk2[:, :B_KV_COLS]
    rk = lax.rsqrt(_group_mean_sq(pk, ones_ref[:B_KV_COLS, :B_KV_COLS]) + NORM_EPS)
    kn = pk * rk * knw_ref[...]
    kn_rot = pk2[:, B_KV_COLS:] * rk * knwp_ref[...]
    kb_ref[0] = (kn * cosb_ref[...] + kn_rot * sinb_ref[...]).astype(BF16)
    vbt = proj(OFF_VB, B_KV_COLS).T.astype(BF16)
    for g in range(B_KV_HEADS):
        vbt_ref[0, 0, g, :B_DIM, :] = vbt[g * B_DIM:(g + 1) * B_DIM]
        vbt_ref[0, 0, g, B_DIM:, :] = ones_rows


def _prep(x, anw, w_fused, tables, qnw, qnwp, knw, knwp, ones, tm):
    b, s, d = x.shape
    nt = s // tm
    cosa, sina, cosb, sinb = tables
    full = lambda shape: pl.BlockSpec(shape, lambda bi, i: (0,) * len(shape))
    tab = pl.BlockSpec((tm, LANES), lambda bi, i: (i, 0))
    tok = lambda n: pl.BlockSpec((1, tm, n), lambda bi, i: (bi, i, 0))
    qt = lambda heads: pl.BlockSpec((1, heads, LANES, tm), lambda bi, i: (bi, 0, 0, i))
    sds = lambda *shape: jax.ShapeDtypeStruct(shape, BF16)
    return pl.pallas_call(
        _prep_kernel,
        grid=(b, nt),
        in_specs=[pl.BlockSpec((1, tm, d), lambda bi, i: (bi, i, 0)),
                  full((1, d)), full((d, FUSED_COLS)), tab, tab, tab, tab,
                  full((1, B_Q_COLS)), full((1, B_Q_COLS)), full((1, B_KV_COLS)), full((1, B_KV_COLS)),
                  full((B_Q_COLS, B_Q_COLS))],
        out_specs=[qt(2 * A_HEADS), tok(A_COLS),
                   pl.BlockSpec((1, A_HEADS, 1, V_ROWS_A, tm), lambda bi, i: (bi, 0, i, 0, 0)),
                   qt(B_HEADS), tok(B_KV_COLS),
                   pl.BlockSpec((1, 1, B_KV_HEADS, V_ROWS_B, tm), lambda bi, i: (bi, i, 0, 0, 0))],
        out_shape=[sds(b, 2 * A_HEADS, LANES, s), sds(b, s, A_COLS), sds(b, A_HEADS, nt, V_ROWS_A, tm),
                   sds(b, B_HEADS, LANES, s), sds(b, s, B_KV_COLS), sds(b, nt, B_KV_HEADS, V_ROWS_B, tm)],
        compiler_params=_params("parallel", "parallel"),
        name="prep",
    )(x, anw, w_fused, cosa, sina, cosb, sinb, qnw, qnwp, knw, knwp, ones)


def _flash_scratch(rows, tk, v_rows):
    return [pltpu.VMEM((1, rows), F32), pltpu.VMEM((v_rows, rows), F32),
            pltpu.VMEM((tk, rows), F32), pltpu.VMEM((tk, rows), F32)]


def _flash_t(q_chunk, n_chunks, ch, k_ref, vt_tile, m_sc, acc_sc, s0_sc, s1_sc):
    m_sc[...] = jnp.full(m_sc.shape, -jnp.inf, F32)
    acc_sc[...] = jnp.zeros(acc_sc.shape, F32)
    tk = s0_sc.shape[0]
    nk = k_ref.shape[1] // tk
    assert nk >= 2 and nk % 2 == 0, "key tiles are processed in pairs"
    chunks = [slice(c * ch, (c + 1) * ch) for c in range(n_chunks)]

    def scores(j, c, dst):
        k = k_ref[0, pl.ds(pl.multiple_of(j * tk, tk), tk), :]
        dst[:, chunks[c]] = jnp.dot(k, q_chunk(c), preferred_element_type=F32)

    def softmax_pv(j, c, src):
        sl = chunks[c]
        s = src[:, sl]
        m_prev = m_sc[:, sl]
        m_new = jnp.maximum(m_prev, jnp.max(s, axis=0, keepdims=True))
        p = jnp.exp(s - m_new).astype(BF16)
        acc_sc[:, sl] = (jnp.exp(m_prev - m_new) * acc_sc[:, sl]
                         + jnp.dot(vt_tile(j, c), p, preferred_element_type=F32))
        m_sc[:, sl] = m_new

    def step(j, src, dst):
        for c in range(n_chunks):
            if dst is not None:
                scores(j + 1, c, dst)
            softmax_pv(j, c, src)

    for c in range(n_chunks):
        scores(0, c, s0_sc)

    def body(i, carry):
        step(2 * i, s0_sc, s1_sc)
        step(2 * i + 1, s1_sc, s0_sc)
        return carry

    lax.fori_loop(0, nk // 2 - 1, body, 0)
    step(nk - 2, s0_sc, s1_sc)
    step(nk - 1, s1_sc, None)
    acc = acc_sc[...]
    n_val = acc.shape[0] - ONES_ROWS
    return acc[:n_val] / acc[n_val:n_val + 1]


def _diff_attn_kernel(qt_ref, k_ref, vt_ref, lq1_ref, lk1_ref, lq2_ref, lk2_ref, sw_ref,
                      o_ref, *scratch, ch):
    tq = qt_ref.shape[3]
    per_map = tq // ch

    def q_chunk(c):
        return qt_ref[0, c // per_map, :, (c % per_map) * ch:(c % per_map + 1) * ch]

    ot = _flash_t(q_chunk, 2 * per_map, ch, k_ref, lambda j, c: vt_ref[0, 0, j], *scratch)
    lam = (jnp.exp(jnp.sum(lq1_ref[...] * lk1_ref[...], axis=1, keepdims=True))
           - jnp.exp(jnp.sum(lq2_ref[...] * lk2_ref[...], axis=1, keepdims=True)) + LAMBDA_INIT)
    d = (ot[:, :tq] - lam * ot[:, tq:]).T
    o_ref[0] = (_rms(d) * sw_ref[...] * (1.0 - LAMBDA_INIT)).astype(BF16)


def _diff_attn(qat, ka, vat, lq1, lk1, lq2, lk2, sw, tq, tk, ch):
    b, s, _ = ka.shape
    small = pl.BlockSpec((1, A_DIM), lambda bi, h, i: (0, 0))
    return pl.pallas_call(
        functools.partial(_diff_attn_kernel, ch=ch),
        grid=(b, A_HEADS, s // tq),
        in_specs=[pl.BlockSpec((1, 2, LANES, tq), lambda bi, h, i: (bi, h, 0, i)),
                  pl.BlockSpec((1, s, LANES), lambda bi, h, i: (bi, 0, h)),
                  pl.BlockSpec((1, 1, s // tk, V_ROWS_A, tk), lambda bi, h, i: (bi, h, 0, 0, 0)),
                  small, small, small, small, pl.BlockSpec((1, A_VDIM), lambda bi, h, i: (0, 0))],
        out_specs=pl.BlockSpec((1, tq, LANES), lambda bi, h, i: (bi, i, h)),
        out_shape=jax.ShapeDtypeStruct((b, s, A_COLS), BF16),
        scratch_shapes=_flash_scratch(2 * tq, tk, V_ROWS_A),
        compiler_params=_params("parallel", "parallel", "arbitrary"),
        name="diff_attn",
    )(qat, ka, vat, lq1, lk1, lq2, lk2, sw)


def _gqa_kernel(qt_ref, k_ref, vt_ref, o_ref, *scratch, ch):
    tq = qt_ref.shape[3]
    n_chunks = B_HEADS * tq // ch
    first_head = lambda c: c * ch // tq

    def q_chunk(c):
        if tq >= ch:
            off = c * ch % tq
            return qt_ref[0, first_head(c), :, off:off + ch]
        return jnp.concatenate([qt_ref[0, first_head(c) + i] for i in range(ch // tq)], axis=1)

    def vt_tile(j, c):
        return vt_ref[0, j, first_head(c) // B_GROUP]

    ot = _flash_t(q_chunk, n_chunks, ch, k_ref, vt_tile, *scratch)
    pairs = [ot[:, (2 * c) * tq:(2 * c + 2) * tq] for c in range(B_HEADS // 2)]
    o_ref[0] = jnp.concatenate(
        [jnp.concatenate([p[:, :tq], p[:, tq:]], axis=0).T for p in pairs], axis=1).astype(BF16)


def _gqa_attn(qbt, kb, vbt, tq, tk, ch):
    b, s, _ = kb.shape
    return pl.pallas_call(
        functools.partial(_gqa_kernel, ch=ch),
        grid=(b, s // tq),
        in_specs=[pl.BlockSpec((1, B_HEADS, LANES, tq), lambda bi, i: (bi, 0, 0, i)),
                  pl.BlockSpec((1, s, LANES), lambda bi, i: (bi, 0, 0)),
                  pl.BlockSpec((1, s // tk, B_KV_HEADS, V_ROWS_B, tk), lambda bi, i: (bi, 0, 0, 0, 0))],
        out_specs=pl.BlockSpec((1, tq, B_Q_COLS), lambda bi, i: (bi, i, 0)),
        out_shape=jax.ShapeDtypeStruct((b, s, B_Q_COLS), BF16),
        scratch_shapes=_flash_scratch(B_HEADS * tq, tk, V_ROWS_B),
        compiler_params=_params("parallel", "arbitrary"),
        name="gqa_attn",
    )(qbt, kb, vbt)


def _mix_kernel(oa_ref, ob_ref, x_ref, wo_ref, fnw_ref, wq_ref, keys_ref, x1_ref, xn_ref, sc_ref):
    y = (jnp.dot(oa_ref[...], wo_ref[:A_COLS, :], preferred_element_type=F32)
         + jnp.dot(ob_ref[...], wo_ref[A_COLS:, :], preferred_element_type=F32))
    x1 = x_ref[...] + y
    xn = _rms(x1) * fnw_ref[...]
    x1_ref[...] = x1
    xn_ref[...] = xn
    xb = xn.astype(BF16)
    for h in range(PEER_HEADS):
        q = jnp.dot(xb, wq_ref[:, h * 2 * PEER_D_HALF:(h + 1) * 2 * PEER_D_HALF],
                    preferred_element_type=F32).astype(BF16)
        for c in range(2):
            sc_ref[2 * h + c] = lax.dot_general(
                keys_ref[2 * h + c], q[:, c * PEER_D_HALF:(c + 1) * PEER_D_HALF], (((1,), (1,)), ((), ())),
                preferred_element_type=F32)


def _mix(oa, ob, x, wo, fnw, wq, keys, tm):
    t, d = x.shape
    nq = wq.shape[1]
    row = lambda n: pl.BlockSpec((tm, n), lambda i: (i, 0))
    full = lambda shape: pl.BlockSpec(shape, lambda i: (0,) * len(shape))
    return pl.pallas_call(
        _mix_kernel,
        grid=(t // tm,),
        in_specs=[row(A_COLS), row(B_Q_COLS), row(d), full((d, d)), full((1, d)), full((d, nq)),
                  full((2 * PEER_HEADS, PEER_N_KEYS, PEER_D_HALF))],
        out_specs=[row(d), row(d),
                   pl.BlockSpec((2 * PEER_HEADS, PEER_N_KEYS, tm), lambda i: (0, 0, i))],
        out_shape=[jax.ShapeDtypeStruct((t, d), F32), jax.ShapeDtypeStruct((t, d), F32),
                   jax.ShapeDtypeStruct((2 * PEER_HEADS, PEER_N_KEYS, t), F32)],
        compiler_params=_params("parallel"),
        name="mix",
    )(oa, ob, x, wo, fnw, wq, keys)


_NO_ROW = float(2 ** 20)


def _topk_rows(x, iota, k):
    vals, rows = [], []
    for _ in range(k):
        m = jnp.max(x, axis=0, keepdims=True)
        r = jnp.min(jnp.where(x == m, iota, _NO_ROW), axis=0, keepdims=True)
        vals.append(m)
        rows.append(r)
        x = jnp.where(iota == r, -jnp.inf, x)
    return jnp.concatenate(vals, axis=0), jnp.concatenate(rows, axis=0)


def _select_rows(table, sel):
    out = jnp.zeros_like(sel)
    for r in range(table.shape[0]):
        out = jnp.where(sel == float(r), table[r:r + 1, :], out)
    return out


def _pair_candidates(s1, s2, sub):
    lo, hi = s2[:SUBLANES], s2[SUBLANES:]
    vals, ids = [s1[0:1] + lo, s1[0:1] + hi], [sub, sub + float(SUBLANES)]
    for a in range(1, SUBLANES):
        bound = PEER_TOPK // (a + 1)
        v = s1[a:a + 1] + lo
        vals.append(v if bound >= SUBLANES else jnp.where(sub < float(bound), v, -jnp.inf))
        ids.append(sub + float(a * PEER_TOPK))
    vals.append(s1[SUBLANES:] + s2[0:1])
    ids.append((sub + float(SUBLANES)) * float(PEER_TOPK))
    return jnp.concatenate(vals, axis=0), jnp.concatenate(ids, axis=0)


def _topk_kernel(sc_ref, idx_ref, g_ref, idx_sc):
    tt = sc_ref.shape[2]
    iota_keys = lax.broadcasted_iota(jnp.int32, (PEER_N_KEYS, tt), 0).astype(F32)
    sub = lax.broadcasted_iota(jnp.int32, (SUBLANES, tt), 0).astype(F32)

    def key_stage(h):
        s1, i1 = _topk_rows(sc_ref[2 * h], iota_keys, PEER_TOPK)
        s2, i2 = _topk_rows(sc_ref[2 * h + 1], iota_keys, PEER_TOPK)
        return s1, i1, s2, i2

    def pair_stage(h, s1, i1, s2, i2):
        top_s, pos = _topk_rows(*_pair_candidates(s1, s2, sub), PEER_TOPK)
        a = jnp.floor(pos * (1.0 / PEER_TOPK))
        b = pos - a * PEER_TOPK
        expert = _select_rows(i1, a) * PEER_N_KEYS + _select_rows(i2, b)
        e = jnp.exp(top_s - top_s[0:1, :])
        off = pl.multiple_of(h * PEER_TOPK, PEER_TOPK)
        idx_sc[pl.ds(off, PEER_TOPK), :] = expert
        g_ref[pl.ds(off, PEER_TOPK), :] = e / jnp.sum(e, axis=0, keepdims=True)

    def body(h, carry):
        nxt = key_stage(h + 1)
        pair_stage(h, *carry)
        return nxt

    last = lax.fori_loop(0, PEER_HEADS - 1, body, key_stage(0))
    pair_stage(PEER_HEADS - 1, *last)
    idx_ref[...] = idx_sc[...].T.astype(jnp.int32)


def _topk(sc, tt):
    t = sc.shape[2]
    return pl.pallas_call(
        _topk_kernel,
        grid=(t // tt,),
        in_specs=[pl.BlockSpec((2 * PEER_HEADS, PEER_N_KEYS, tt), lambda i: (0, 0, i))],
        out_specs=[pl.BlockSpec((tt, PEER_PICKS), lambda i: (i, 0)),
                   pl.BlockSpec((PEER_PICKS, tt), lambda i: (0, i))],
        out_shape=[jax.ShapeDtypeStruct((t, PEER_PICKS), jnp.int32),
                   jax.ShapeDtypeStruct((PEER_PICKS, t), F32)],
        scratch_shapes=[pltpu.VMEM((PEER_PICKS, tt), F32)],
        compiler_params=_params("parallel"),
        name="topk",
    )(sc)


def _gelu(x):
    return 0.5 * x * (1.0 + lax.erf(x * (1.0 / math.sqrt(2.0))))


PEER_SLOTS = 8
PEER_BATCH = 2
PEER_AHEAD = PEER_SLOTS - PEER_BATCH


def _sublane_fold(vregs):
    sub = lax.broadcasted_iota(jnp.int32, (SUBLANES, LANES), 0)
    level, shift = list(vregs), SUBLANES // 2
    while len(level) > 1:
        keep_first = (sub & shift) == 0
        half = len(level) // 2
        level = [jnp.where(keep_first,
                           level[i] + pltpu.roll(level[i], SUBLANES - shift, axis=0),
                           level[i + half] + pltpu.roll(level[i + half], shift, axis=0))
                 for i in range(half)]
        shift //= 2
    return level[0]


def _peer_kernel(idx_ref, g_ref, xn_ref, x1_ref, fw_ref, uv_hbm, o_ref, *scratch):
    bufs, (w_sc, ids, sem, id_sem) = scratch[:PEER_SLOTS], scratch[PEER_SLOTS:]
    tb = xn_ref.shape[0]
    assert tb % PEER_SLOTS == 0 and tb >= 2 * PEER_SLOTS
    tok0 = pl.program_id(0) * tb

    def id_copy(t, slot):
        row = tok0 + jnp.minimum(t, tb - 1)
        return pltpu.make_async_copy(idx_ref.at[row], ids.at[slot], id_sem.at[slot])

    def issue(t, slot):
        for j in range(PEER_PICKS):
            pltpu.make_async_copy(uv_hbm.at[ids[slot, j]], bufs[slot].at[j], sem.at[slot]).start(priority=j % 2)
        id_copy(t + PEER_SLOTS, slot).start()

    def wait(slot):
        pltpu.make_async_copy(uv_hbm.at[pl.ds(0, PEER_PICKS)], bufs[slot], sem.at[slot]).wait()

    lane = lax.broadcasted_iota(jnp.int32, (PEER_PICKS, tb), 1)

    def compute(t, slot):
        buf = bufs[slot]
        x = xn_ref[t]
        acts = []
        for g in range(PEER_PICKS // SUBLANES):
            prods = [buf[g * SUBLANES + k, :, :LANES] * x for k in range(SUBLANES)]
            acts.append(jnp.sum(_sublane_fold(prods), axis=1, keepdims=True))
        act = jnp.concatenate(acts, axis=0)
        gate = jnp.sum(jnp.where(lane == t, g_ref[...], 0.0), axis=1, keepdims=True)
        w_sc[slot] = jnp.broadcast_to(gate * _gelu(act), (PEER_PICKS, LANES))
        partial = [jnp.zeros((SUBLANES, LANES), F32) for _ in range(4)]
        for j in range(PEER_PICKS):
            partial[j % 4] = partial[j % 4] + (jnp.broadcast_to(w_sc[slot, pl.ds(j, 1), :], (SUBLANES, LANES))
                                               * buf[j, :, LANES:])
        r = x1_ref[t] + ((partial[0] + partial[1]) + (partial[2] + partial[3]))
        ms = jnp.sum(jnp.sum(r * r, axis=1, keepdims=True), axis=0, keepdims=True) * (1.0 / D_MODEL)
        o_ref[t] = r * lax.rsqrt(ms + NORM_EPS) * fw_ref[...]

    def group(base, last):
        for s0 in range(0, PEER_SLOTS, PEER_BATCH):
            batch = range(s0, s0 + PEER_BATCH)
            ahead = [s for s in batch if not last or s + PEER_AHEAD < PEER_SLOTS]
            for s in batch:
                wait(s)
            for s in ahead:
                id_copy(0, (s + PEER_AHEAD) % PEER_SLOTS).wait()
            for s in ahead:
                issue(base + s + PEER_AHEAD, (s + PEER_AHEAD) % PEER_SLOTS)
            for s in batch:
                compute(base + s, s)

    for s in range(PEER_SLOTS):
        id_copy(s, s).start()
    for s in range(PEER_AHEAD):
        id_copy(0, s).wait()
    for s in range(PEER_AHEAD):
        issue(s, s)

    def steady(i, carry):
        group(i * PEER_SLOTS, False)
        return carry

    lax.fori_loop(0, tb // PEER_SLOTS - 1, steady, 0)
    group(tb - PEER_SLOTS, True)
    for s in range(PEER_SLOTS):
        id_copy(0, s).wait()


def _as_vregs(a):
    return a.reshape(*a.shape[:-1], SUBLANES, LANES)


def _peer(idx, g, xn, x1, fw, uv, tb):
    t, d = xn.shape
    tok = pl.BlockSpec((tb, SUBLANES, LANES), lambda i: (i, 0, 0))
    out = pl.pallas_call(
        _peer_kernel,
        grid=(t // tb,),
        in_specs=[pl.BlockSpec(memory_space=pl.ANY),
                  pl.BlockSpec((PEER_PICKS, tb), lambda i: (0, i)),
                  tok, tok, pl.BlockSpec((SUBLANES, LANES), lambda i: (0, 0)),
                  pl.BlockSpec(memory_space=pl.ANY)],
        out_specs=tok,
        out_shape=jax.ShapeDtypeStruct((t, SUBLANES, LANES), F32),
        scratch_shapes=([pltpu.VMEM((PEER_PICKS, SUBLANES, 2 * LANES), F32) for _ in range(PEER_SLOTS)]
                        + [pltpu.VMEM((PEER_SLOTS, PEER_PICKS, LANES), F32),
                           pltpu.SMEM((PEER_SLOTS, PEER_PICKS), jnp.int32),
                           pltpu.SemaphoreType.DMA((PEER_SLOTS,)), pltpu.SemaphoreType.DMA((PEER_SLOTS,))]),
        compiler_params=_params("arbitrary"),
        name="peer",
    )(idx, g, _as_vregs(xn), _as_vregs(x1), _as_vregs(fw)[0], uv)
    return out.reshape(t, d)


def _forward(x, p):
    b, s, d = x.shape
    tk = min(512, s // 2)
    ch = 2 * LANES
    qat, ka, vat, qbt, kb, vbt = _prep(x, p["anw"], p["w_fused"], _position_tables(s),
                                       p["qnw"], p["qnwp"], p["knw"], p["knwp"], p["ones"], tk)
    oa = _diff_attn(qat, ka, vat, p["lq1"], p["lk1"], p["lq2"], p["lk2"], p["sw"], min(2048, s), tk, ch)
    ob = _gqa_attn(qbt, kb, vbt, min(4 * LANES, s), tk, ch)
    t = b * s
    x1, xn, sc = _mix(oa.reshape(t, A_COLS), ob.reshape(t, B_Q_COLS), x.reshape(t, d),
                      p["wo"], p["fnw"], p["wq"], p["keys"], min(512, t))
    idx, g = _topk(sc, LANES)
    y = _peer(idx, g, xn, x1, p["fw"], p["uv"], min(2 * LANES, t))
    return y.reshape(b, s, d)


def kernel(x_prompt, x_sample, attn_norm, w_in, lambda_q1, lambda_k1, lambda_q2, lambda_k2, subln_w,
           q_norm_w, k_norm_w, w_out, ffn_norm, peer_wq, peer_keys, peer_u, peer_v, final_norm):
    perm_b, _ = _rotate_half_perm(B_Q_COLS, B_DIM // 2)
    qnw = jnp.tile(q_norm_w[0], B_HEADS)
    knw = jnp.tile(k_norm_w[0], B_KV_HEADS)
    p = dict(
        anw=attn_norm[0][None, :],
        w_fused=_fused_in_weight(w_in[0]),
        qnw=qnw[None, :], qnwp=qnw[perm_b][None, :],
        knw=knw[None, :], knwp=knw[perm_b[:B_KV_COLS]][None, :],
        ones=_group_ones(B_Q_COLS, B_DIM),
        lq1=lambda_q1[0][None, :], lk1=lambda_k1[0][None, :],
        lq2=lambda_q2[0][None, :], lk2=lambda_k2[0][None, :],
        sw=subln_w[0][None, :],
        wo=w_out[0].astype(BF16), fnw=ffn_norm[0][None, :],
        wq=peer_wq[0].astype(BF16),
        keys=peer_keys[0].reshape(2 * PEER_HEADS, PEER_N_KEYS, PEER_D_HALF).astype(BF16),
        uv=jnp.concatenate([_as_vregs(peer_u[0]), _as_vregs(peer_v[0])], axis=2), fw=final_norm[None, :],
    )
    return _forward(x_prompt, p), _forward(x_sample, p)
```

```python
import functools
import math

import numpy as np
import jax
import jax.numpy as jnp
from jax import lax
from jax.experimental import pallas as pl
from jax.experimental.pallas import tpu as pltpu

F32 = jnp.float32
BF16 = jnp.bfloat16

D_MODEL = 1024
A_HEADS = 4
A_DIM = 64
A_VDIM = 2 * A_DIM
B_HEADS = 8
B_KV_HEADS = 2
B_DIM = 64
B_GROUP = B_HEADS // B_KV_HEADS
A_COLS = A_HEADS * 2 * A_DIM
B_Q_COLS = B_HEADS * B_DIM
B_KV_COLS = B_KV_HEADS * B_DIM
ROPE_THETA = 10000.0
GRID_W = 64
NORM_EPS = 1e-6
LAMBDA_INIT = 0.8 - 0.6 * math.exp(-0.3)
PEER_HEADS = 8
PEER_N_KEYS = 128
PEER_D_HALF = 128
PEER_TOPK = 16
PEER_PICKS = PEER_HEADS * PEER_TOPK

LANES = 128
SUBLANES = 8
HALF = LANES // 2
BF16_SUBLANES = 16
ONES_ROWS = BF16_SUBLANES
V_ROWS_A = A_VDIM + ONES_ROWS
V_ROWS_B = B_DIM + ONES_ROWS
VMEM_LIMIT_BYTES = 56 * 1024 * 1024

OFF_QA, OFF_QA_ROT = 0, 512
OFF_KA, OFF_KA_ROT = 1024, 1536
OFF_VA = 2048
OFF_QB, OFF_QB_ROT = 2560, 3072
OFF_KB, OFF_KB_ROT = 3584, 3712
OFF_VB = 3840
FUSED_COLS = 3968


def _params(*semantics):
    return pltpu.CompilerParams(dimension_semantics=semantics, vmem_limit_bytes=VMEM_LIMIT_BYTES)


def _rope_tables(pos, dim):
    inv = ROPE_THETA ** (-jnp.arange(0, dim, 2, dtype=F32) / dim)
    ang = pos[:, None] * inv[None, :]
    ang = jnp.concatenate([ang, ang], axis=-1)
    return jnp.cos(ang), jnp.sin(ang)


def _position_tables(s):
    rows = s // GRID_W
    t = jnp.arange(s, dtype=F32)
    row = jnp.broadcast_to(jnp.arange(rows, dtype=F32)[:, None], (rows, GRID_W)).reshape(s)
    col = jnp.broadcast_to(jnp.arange(GRID_W, dtype=F32)[None, :], (rows, GRID_W)).reshape(s)
    cos_a, sin_a = _rope_tables(t, A_DIM)
    cos_r, sin_r = _rope_tables(row, B_DIM // 2)
    cos_c, sin_c = _rope_tables(col, B_DIM // 2)
    cos_b = jnp.concatenate([cos_r, cos_c], axis=-1)
    sin_b = jnp.concatenate([sin_r, sin_c], axis=-1)
    two = lambda a: jnp.concatenate([a, a], axis=-1)
    return two(cos_a), two(sin_a), two(cos_b), two(sin_b)


def _rotate_half_perm(n_cols, block):
    i = np.arange(n_cols)
    o = i % block
    half = block // 2
    perm = np.where(o < half, i + half, i - half)
    sign = np.where(o < half, -1.0, 1.0).astype(np.float32)
    return perm, sign


def _fused_in_weight(w_in):
    wqa = w_in[:, 0:512]
    wka = w_in[:, 512:1024]
    wva = w_in[:, 1024:1536]
    wqb = w_in[:, 1536:2048]
    wkb = w_in[:, 2048:2176]
    wvb = w_in[:, 2176:2304]

    def rot(m, block):
        perm, sign = _rotate_half_perm(m.shape[1], block)
        return m[:, perm] * sign

    parts = [wqa, rot(wqa, A_DIM), wka, rot(wka, A_DIM), wva,
             wqb, rot(wqb, B_DIM // 2), wkb, rot(wkb, B_DIM // 2), wvb]
    return jnp.concatenate(parts, axis=1).astype(BF16)


def _group_ones(n, group):
    i = np.arange(n) // group
    return jnp.asarray((i[:, None] == i[None, :]).astype(np.float32), dtype=BF16)


def _rms(x):
    return x * lax.rsqrt(jnp.mean(x * x, axis=-1, keepdims=True) + NORM_EPS)


def _group_mean_sq(p, ones):
    sq = p * p
    hi = sq.astype(BF16)
    lo = (sq - hi.astype(F32)).astype(BF16)
    ss = jnp.dot(hi, ones, preferred_element_type=F32) + jnp.dot(lo, ones, preferred_element_type=F32)
    return ss * (1.0 / B_DIM)


def _prep_kernel(x_ref, anw_ref, w_ref, cosa_ref, sina_ref, cosb_ref, sinb_ref,
                 qnw_ref, qnwp_ref, knw_ref, knwp_ref, ones_ref,
                 qat_ref, ka_ref, vat_ref, qbt_ref, kb_ref, vbt_ref):
    x = x_ref[0]
    tm = x.shape[0]
    hb = (_rms(x) * anw_ref[...]).astype(BF16)

    def proj(off, n):
        return jnp.dot(hb, w_ref[:, off:off + n], preferred_element_type=F32)

    def tile_lanes(a, n):
        return jnp.concatenate([a] * n, axis=1)

    zeros_half = jnp.zeros((HALF, tm), F32)
    ones_rows = jnp.ones((ONES_ROWS, tm), BF16)

    cosa = tile_lanes(cosa_ref[...], A_COLS // LANES)
    sina = tile_lanes(sina_ref[...], A_COLS // LANES)
    qa = (proj(OFF_QA, A_COLS) * cosa + proj(OFF_QA_ROT, A_COLS) * sina) * (A_DIM ** -0.5)
    ka = proj(OFF_KA, A_COLS) * cosa + proj(OFF_KA_ROT, A_COLS) * sina
    va = proj(OFF_VA, A_COLS)
    for h in range(A_HEADS):
        qt = qa[:, h * LANES:(h + 1) * LANES].T
        qat_ref[0, 2 * h] = jnp.concatenate([qt[:HALF], zeros_half], axis=0).astype(BF16)
        qat_ref[0, 2 * h + 1] = jnp.concatenate([zeros_half, qt[HALF:]], axis=0).astype(BF16)
        vat_ref[0, h, 0, :LANES, :] = va[:, h * LANES:(h + 1) * LANES].T.astype(BF16)
        vat_ref[0, h, 0, LANES:, :] = ones_rows
    ka_ref[0] = ka.astype(BF16)

    cosb = tile_lanes(cosb_ref[...], B_Q_COLS // LANES)
    sinb = tile_lanes(sinb_ref[...], B_Q_COLS // LANES)
    pq = proj(OFF_QB, B_Q_COLS)
    rq = lax.rsqrt(_group_mean_sq(pq, ones_ref[...]) + NORM_EPS)
    qn = pq * rq * qnw_ref[...]
    qn_rot = proj(OFF_QB_ROT, B_Q_COLS) * rq * qnwp_ref[...]
    qb = (qn * cosb + qn_rot * sinb) * (B_DIM ** -0.5)
    for c in range(B_HEADS // 2):
        qt = qb[:, c * LANES:(c + 1) * LANES].T
        for parity in range(2):
            h = 2 * c + parity
            piece = qt[parity * HALF:(parity + 1) * HALF]
            parts = [piece, zeros_half] if h // B_GROUP == 0 else [zeros_half, piece]
            qbt_ref[0, h] = jnp.concatenate(parts, axis=0).astype(BF16)

    assert OFF_KB_ROT == OFF_KB + B_KV_COLS
    pk2 = proj(OFF_KB, 2 * B_KV_COLS)
    pk = pk2[:, :B_KV_COLS]
    rk = lax.rsqrt(_group_mean_sq(pk, ones_ref[:B_KV_COLS, :B_KV_COLS]) + NORM_EPS)
    kn = pk * rk * knw_ref[...]
    kn_rot = pk2[:, B_KV_COLS:] * rk * knwp_ref[...]
    kb_ref[0] = (kn * cosb_ref[...] + kn_rot * sinb_ref[...]).astype(BF16)
    vbt = proj(OFF_VB, B_KV_COLS).T.astype(BF16)
    for g in range(B_KV_HEADS):
        vbt_ref[0, 0, g, :B_DIM, :] = vbt[g * B_DIM:(g + 1) * B_DIM]
        vbt_ref[0, 0, g, B_DIM:, :] = ones_rows


def _prep(x, anw, w_fused, tables, qnw, qnwp, knw, knwp, ones, tm):
    b, s, d = x.shape
    nt = s // tm
    cosa, sina, cosb, sinb = tables
    full = lambda shape: pl.BlockSpec(shape, lambda bi, i: (0,) * len(shape))
    tab = pl.BlockSpec((tm, LANES), lambda bi, i: (i, 0))
    tok = lambda n: pl.BlockSpec((1, tm, n), lambda bi, i: (bi, i, 0))
    qt = lambda heads: pl.BlockSpec((1, heads, LANES, tm), lambda bi, i: (bi, 0, 0, i))
    sds = lambda *shape: jax.ShapeDtypeStruct(shape, BF16)
    return pl.pallas_call(
        _prep_kernel,
        grid=(b, nt),
        in_specs=[pl.BlockSpec((1, tm, d), lambda bi, i: (bi, i, 0)),
                  full((1, d)), full((d, FUSED_COLS)), tab, tab, tab, tab,
                  full((1, B_Q_COLS)), full((1, B_Q_COLS)), full((1, B_KV_COLS)), full((1, B_KV_COLS)),
                  full((B_Q_COLS, B_Q_COLS))],
        out_specs=[qt(2 * A_HEADS), tok(A_COLS),
                   pl.BlockSpec((1, A_HEADS, 1, V_ROWS_A, tm), lambda bi, i: (bi, 0, i, 0, 0)),
                   qt(B_HEADS), tok(B_KV_COLS),
                   pl.BlockSpec((1, 1, B_KV_HEADS, V_ROWS_B, tm), lambda bi, i: (bi, i, 0, 0, 0))],
        out_shape=[sds(b, 2 * A_HEADS, LANES, s), sds(b, s, A_COLS), sds(b, A_HEADS, nt, V_ROWS_A, tm),
                   sds(b, B_HEADS, LANES, s), sds(b, s, B_KV_COLS), sds(b, nt, B_KV_HEADS, V_ROWS_B, tm)],
        compiler_params=_params("parallel", "parallel"),
        name="prep",
    )(x, anw, w_fused, cosa, sina, cosb, sinb, qnw, qnwp, knw, knwp, ones)


def _flash_scratch(rows, tk, v_rows):
    return [pltpu.VMEM((1, rows), F32), pltpu.VMEM((v_rows, rows), F32),
            pltpu.VMEM((tk, rows), F32), pltpu.VMEM((tk, rows), F32)]


def _flash_t(q_chunk, n_chunks, ch, k_ref, vt_tile, m_sc, acc_sc, s0_sc, s1_sc):
    m_sc[...] = jnp.full(m_sc.shape, -jnp.inf, F32)
    acc_sc[...] = jnp.zeros(acc_sc.shape, F32)
    tk = s0_sc.shape[0]
    nk = k_ref.shape[1] // tk
    assert nk >= 2 and nk % 2 == 0, "key tiles are processed in pairs"
    chunks = [slice(c * ch, (c + 1) * ch) for c in range(n_chunks)]

    def scores(j, c, dst):
        k = k_ref[0, pl.ds(pl.multiple_of(j * tk, tk), tk), :]
        dst[:, chunks[c]] = jnp.dot(k, q_chunk(c), preferred_element_type=F32)

    def softmax_pv(j, c, src):
        sl = chunks[c]
        s = src[:, sl]
        m_prev = m_sc[:, sl]
        m_new = jnp.maximum(m_prev, jnp.max(s, axis=0, keepdims=True))
        p = jnp.exp(s - m_new).astype(BF16)
        acc_sc[:, sl] = (jnp.exp(m_prev - m_new) * acc_sc[:, sl]
                         + jnp.dot(vt_tile(j, c), p, preferred_element_type=F32))
        m_sc[:, sl] = m_new

    def step(j, src, dst):
        for c in range(n_chunks):
            if dst is not None:
                scores(j + 1, c, dst)
            softmax_pv(j, c, src)

    for c in range(n_chunks):
        scores(0, c, s0_sc)

    def body(i, carry):
        step(2 * i, s0_sc, s1_sc)
        step(2 * i + 1, s1_sc, s0_sc)
        return carry

    lax.fori_loop(0, nk // 2 - 1, body, 0)
    step(nk - 2, s0_sc, s1_sc)
    step(nk - 1, s1_sc, None)
    acc = acc_sc[...]
    n_val = acc.shape[0] - ONES_ROWS
    return acc[:n_val] / acc[n_val:n_val + 1]


def _diff_attn_kernel(qt_ref, k_ref, vt_ref, lq1_ref, lk1_ref, lq2_ref, lk2_ref, sw_ref,
                      o_ref, *scratch, ch):
    tq = qt_ref.shape[3]
    per_map = tq // ch

    def q_chunk(c):
        return qt_ref[0, c // per_map, :, (c % per_map) * ch:(c % per_map + 1) * ch]

    ot = _flash_t(q_chunk, 2 * per_map, ch, k_ref, lambda j, c: vt_ref[0, 0, j], *scratch)
    lam = (jnp.exp(jnp.sum(lq1_ref[...] * lk1_ref[...], axis=1, keepdims=True))
           - jnp.exp(jnp.sum(lq2_ref[...] * lk2_ref[...], axis=1, keepdims=True)) + LAMBDA_INIT)
    d = (ot[:, :tq] - lam * ot[:, tq:]).T
    o_ref[0] = (_rms(d) * sw_ref[...] * (1.0 - LAMBDA_INIT)).astype(BF16)


def _diff_attn(qat, ka, vat, lq1, lk1, lq2, lk2, sw, tq, tk, ch):
    b, s, _ = ka.shape
    small = pl.BlockSpec((1, A_DIM), lambda bi, h, i: (0, 0))
    return pl.pallas_call(
        functools.partial(_diff_attn_kernel, ch=ch),
        grid=(b, A_HEADS, s // tq),
        in_specs=[pl.BlockSpec((1, 2, LANES, tq), lambda bi, h, i: (bi, h, 0, i)),
                  pl.BlockSpec((1, s, LANES), lambda bi, h, i: (bi, 0, h)),
                  pl.BlockSpec((1, 1, s // tk, V_ROWS_A, tk), lambda bi, h, i: (bi, h, 0, 0, 0)),
                  small, small, small, small, pl.BlockSpec((1, A_VDIM), lambda bi, h, i: (0, 0))],
        out_specs=pl.BlockSpec((1, tq, LANES), lambda bi, h, i: (bi, i, h)),
        out_shape=jax.ShapeDtypeStruct((b, s, A_COLS), BF16),
        scratch_shapes=_flash_scratch(2 * tq, tk, V_ROWS_A),
        compiler_params=_params("parallel", "parallel", "arbitrary"),
        name="diff_attn",
    )(qat, ka, vat, lq1, lk1, lq2, lk2, sw)


def _gqa_kernel(qt_ref, k_ref, vt_ref, o_ref, *scratch, ch):
    tq = qt_ref.shape[3]
    n_chunks = B_HEADS * tq // ch
    first_head = lambda c: c * ch // tq

    def q_chunk(c):
        if tq >= ch:
            off = c * ch % tq
            return qt_ref[0, first_head(c), :, off:off + ch]
        return jnp.concatenate([qt_ref[0, first_head(c) + i] for i in range(ch // tq)], axis=1)

    def vt_tile(j, c):
        return vt_ref[0, j, first_head(c) // B_GROUP]

    ot = _flash_t(q_chunk, n_chunks, ch, k_ref, vt_tile, *scratch)
    pairs = [ot[:, (2 * c) * tq:(2 * c + 2) * tq] for c in range(B_HEADS // 2)]
    o_ref[0] = jnp.concatenate(
        [jnp.concatenate([p[:, :tq], p[:, tq:]], axis=0).T for p in pairs], axis=1).astype(BF16)


def _gqa_attn(qbt, kb, vbt, tq, tk, ch):
    b, s, _ = kb.shape
    return pl.pallas_call(
        functools.partial(_gqa_kernel, ch=ch),
        grid=(b, s // tq),
        in_specs=[pl.BlockSpec((1, B_HEADS, LANES, tq), lambda bi, i: (bi, 0, 0, i)),
                  pl.BlockSpec((1, s, LANES), lambda bi, i: (bi, 0, 0)),
                  pl.BlockSpec((1, s // tk, B_KV_HEADS, V_ROWS_B, tk), lambda bi, i: (bi, 0, 0, 0, 0))],
        out_specs=pl.BlockSpec((1, tq, B_Q_COLS), lambda bi, i: (bi, i, 0)),
        out_shape=jax.ShapeDtypeStruct((b, s, B_Q_COLS), BF16),
        scratch_shapes=_flash_scratch(B_HEADS * tq, tk, V_ROWS_B),
        compiler_params=_params("parallel", "arbitrary"),
        name="gqa_attn",
    )(qbt, kb, vbt)


def _mix_kernel(oa_ref, ob_ref, x_ref, wo_ref, fnw_ref, wq_ref, keys_ref, x1_ref, xn_ref, sc_ref):
    y = (jnp.dot(oa_ref[...], wo_ref[:A_COLS, :], preferred_element_type=F32)
         + jnp.dot(ob_ref[...], wo_ref[A_COLS:, :], preferred_element_type=F32))
    x1 = x_ref[...] + y
    xn = _rms(x1) * fnw_ref[...]
    x1_ref[...] = x1
    xn_ref[...] = xn
    xb = xn.astype(BF16)
    qs = [jnp.dot(xb, wq_ref[:, h * 2 * PEER_D_HALF:(h + 1) * 2 * PEER_D_HALF],
                  preferred_element_type=F32).astype(BF16) for h in range(PEER_HEADS)]
    for h in range(PEER_HEADS):
        for c in range(2):
            sc_ref[2 * h + c] = lax.dot_general(
                keys_ref[2 * h + c], qs[h][:, c * PEER_D_HALF:(c + 1) * PEER_D_HALF], (((1,), (1,)), ((), ())),
                preferred_element_type=F32)


def _mix(oa, ob, x, wo, fnw, wq, keys, tm):
    t, d = x.shape
    nq = wq.shape[1]
    row = lambda n: pl.BlockSpec((tm, n), lambda i: (i, 0))
    full = lambda shape: pl.BlockSpec(shape, lambda i: (0,) * len(shape))
    return pl.pallas_call(
        _mix_kernel,
        grid=(t // tm,),
        in_specs=[row(A_COLS), row(B_Q_COLS), row(d), full((d, d)), full((1, d)), full((d, nq)),
                  full((2 * PEER_HEADS, PEER_N_KEYS, PEER_D_HALF))],
        out_specs=[row(d), row(d),
                   pl.BlockSpec((2 * PEER_HEADS, PEER_N_KEYS, tm), lambda i: (0, 0, i))],
        out_shape=[jax.ShapeDtypeStruct((t, d), F32), jax.ShapeDtypeStruct((t, d), F32),
                   jax.ShapeDtypeStruct((2 * PEER_HEADS, PEER_N_KEYS, t), F32)],
        compiler_params=_params("parallel"),
        name="mix",
    )(oa, ob, x, wo, fnw, wq, keys)


_NO_ROW = float(2 ** 20)


def _topk_rows(x, iota, k):
    vals, rows = [], []
    for _ in range(k):
        m = jnp.max(x, axis=0, keepdims=True)
        r = jnp.min(jnp.where(x == m, iota, _NO_ROW), axis=0, keepdims=True)
        vals.append(m)
        rows.append(r)
        x = jnp.where(iota == r, -jnp.inf, x)
    return jnp.concatenate(vals, axis=0), jnp.concatenate(rows, axis=0)


def _select_rows(table, sel):
    out = jnp.zeros_like(sel)
    for r in range(table.shape[0]):
        out = jnp.where(sel == float(r), table[r:r + 1, :], out)
    return out


def _pair_candidates(s1, s2, sub):
    lo, hi = s2[:SUBLANES], s2[SUBLANES:]
    vals, ids = [s1[0:1] + lo, s1[0:1] + hi], [sub, sub + float(SUBLANES)]
    for a in range(1, SUBLANES):
        bound = PEER_TOPK // (a + 1)
        v = s1[a:a + 1] + lo
        vals.append(v if bound >= SUBLANES else jnp.where(sub < float(bound), v, -jnp.inf))
        ids.append(sub + float(a * PEER_TOPK))
    vals.append(s1[SUBLANES:] + s2[0:1])
    ids.append((sub + float(SUBLANES)) * float(PEER_TOPK))
    return jnp.concatenate(vals, axis=0), jnp.concatenate(ids, axis=0)


def _topk_kernel(sc_ref, idx_ref, g_ref, idx_sc):
    tt = sc_ref.shape[2]
    iota_keys = lax.broadcasted_iota(jnp.int32, (PEER_N_KEYS, tt), 0).astype(F32)
    sub = lax.broadcasted_iota(jnp.int32, (SUBLANES, tt), 0).astype(F32)

    def key_stage(h):
        s1, i1 = _topk_rows(sc_ref[2 * h], iota_keys, PEER_TOPK)
        s2, i2 = _topk_rows(sc_ref[2 * h + 1], iota_keys, PEER_TOPK)
        return s1, i1, s2, i2

    def pair_stage(h, s1, i1, s2, i2):
        top_s, pos = _topk_rows(*_pair_candidates(s1, s2, sub), PEER_TOPK)
        a = jnp.floor(pos * (1.0 / PEER_TOPK))
        b = pos - a * PEER_TOPK
        expert = _select_rows(i1, a) * PEER_N_KEYS + _select_rows(i2, b)
        e = jnp.exp(top_s - top_s[0:1, :])
        off = pl.multiple_of(h * PEER_TOPK, PEER_TOPK)
        idx_sc[pl.ds(off, PEER_TOPK), :] = expert
        g_ref[pl.ds(off, PEER_TOPK), :] = e / jnp.sum(e, axis=0, keepdims=True)

    def body(h, carry):
        nxt = key_stage(h + 1)
        pair_stage(h, *carry)
        return nxt

    last = lax.fori_loop(0, PEER_HEADS - 1, body, key_stage(0))
    pair_stage(PEER_HEADS - 1, *last)
    idx_ref[...] = idx_sc[...].T.astype(jnp.int32)


def _topk(sc, tt):
    t = sc.shape[2]
    return pl.pallas_call(
        _topk_kernel,
        grid=(t // tt,),
        in_specs=[pl.BlockSpec((2 * PEER_HEADS, PEER_N_KEYS, tt), lambda i: (0, 0, i))],
        out_specs=[pl.BlockSpec((tt, PEER_PICKS), lambda i: (i, 0)),
                   pl.BlockSpec((PEER_PICKS, tt), lambda i: (0, i))],
        out_shape=[jax.ShapeDtypeStruct((t, PEER_PICKS), jnp.int32),
                   jax.ShapeDtypeStruct((PEER_PICKS, t), F32)],
        scratch_shapes=[pltpu.VMEM((PEER_PICKS, tt), F32)],
        compiler_params=_params("parallel"),
        name="topk",
    )(sc)


def _gelu(x):
    return 0.5 * x * (1.0 + lax.erf(x * (1.0 / math.sqrt(2.0))))


PEER_SLOTS = 8
PEER_BATCH = 2
PEER_AHEAD = PEER_SLOTS - PEER_BATCH


def _sublane_fold(vregs):
    sub = lax.broadcasted_iota(jnp.int32, (SUBLANES, LANES), 0)
    level, shift = list(vregs), SUBLANES // 2
    while len(level) > 1:
        keep_first = (sub & shift) == 0
        half = len(level) // 2
        level = [jnp.where(keep_first,
                           level[i] + pltpu.roll(level[i], SUBLANES - shift, axis=0),
                           level[i + half] + pltpu.roll(level[i + half], shift, axis=0))
                 for i in range(half)]
        shift //= 2
    return level[0]


def _peer_kernel(idx_ref, g_ref, xn_ref, x1_ref, fw_ref, uv_hbm, o_ref, *scratch):
    bufs, (w_sc, ids, sem, id_sem) = scratch[:PEER_SLOTS], scratch[PEER_SLOTS:]
    tb = xn_ref.shape[0]
    assert tb % PEER_SLOTS == 0 and tb >= 2 * PEER_SLOTS
    tok0 = pl.program_id(0) * tb

    def id_copy(t, slot):
        row = tok0 + jnp.minimum(t, tb - 1)
        return pltpu.make_async_copy(idx_ref.at[row], ids.at[slot], id_sem.at[slot])

    def issue(t, slot):
        for j in range(PEER_PICKS):
            pltpu.make_async_copy(uv_hbm.at[ids[slot, j]], bufs[slot].at[j], sem.at[slot]).start(priority=j % 2)
        id_copy(t + PEER_SLOTS, slot).start()

    def wait(slot):
        pltpu.make_async_copy(uv_hbm.at[pl.ds(0, PEER_PICKS)], bufs[slot], sem.at[slot]).wait()

    lane = lax.broadcasted_iota(jnp.int32, (PEER_PICKS, tb), 1)

    def compute(t, slot):
        buf = bufs[slot]
        x = xn_ref[t]
        acts = []
        for g in range(PEER_PICKS // SUBLANES):
            prods = [buf[g * SUBLANES + k, :, :LANES] * x for k in range(SUBLANES)]
            acts.append(jnp.sum(_sublane_fold(prods), axis=1, keepdims=True))
        act = jnp.concatenate(acts, axis=0)
        gate = jnp.sum(jnp.where(lane == t, g_ref[...], 0.0), axis=1, keepdims=True)
        w_sc[slot] = jnp.broadcast_to(gate * _gelu(act), (PEER_PICKS, LANES))
        partial = [jnp.zeros((SUBLANES, LANES), F32) for _ in range(4)]
        for j in range(PEER_PICKS):
            partial[j % 4] = partial[j % 4] + (jnp.broadcast_to(w_sc[slot, pl.ds(j, 1), :], (SUBLANES, LANES))
                                               * buf[j, :, LANES:])
        r = x1_ref[t] + ((partial[0] + partial[1]) + (partial[2] + partial[3]))
        ms = jnp.sum(jnp.sum(r * r, axis=1, keepdims=True), axis=0, keepdims=True) * (1.0 / D_MODEL)
        o_ref[t] = r * lax.rsqrt(ms + NORM_EPS) * fw_ref[...]

    def group(base, last):
        for s0 in range(0, PEER_SLOTS, PEER_BATCH):
            batch = range(s0, s0 + PEER_BATCH)
            ahead = [s for s in batch if not last or s + PEER_AHEAD < PEER_SLOTS]
            for s in batch:
                wait(s)
            for s in ahead:
                id_copy(0, (s + PEER_AHEAD) % PEER_SLOTS).wait()
            for s in ahead:
                issue(base + s + PEER_AHEAD, (s + PEER_AHEAD) % PEER_SLOTS)
            for s in batch:
                compute(base + s, s)

    for s in range(PEER_SLOTS):
        id_copy(s, s).start()
    for s in range(PEER_AHEAD):
        id_copy(0, s).wait()
    for s in range(PEER_AHEAD):
        issue(s, s)

    def steady(i, carry):
        group(i * PEER_SLOTS, False)
        return carry

    lax.fori_loop(0, tb // PEER_SLOTS - 1, steady, 0)
    group(tb - PEER_SLOTS, True)
    for s in range(PEER_SLOTS):
        id_copy(0, s).wait()


def _as_vregs(a):
    return a.reshape(*a.shape[:-1], SUBLANES, LANES)


def _peer(idx, g, xn, x1, fw, uv, tb):
    t, d = xn.shape
    tok = pl.BlockSpec((tb, SUBLANES, LANES), lambda i: (i, 0, 0))
    out = pl.pallas_call(
        _peer_kernel,
        grid=(t // tb,),
        in_specs=[pl.BlockSpec(memory_space=pl.ANY),
                  pl.BlockSpec((PEER_PICKS, tb), lambda i: (0, i)),
                  tok, tok, pl.BlockSpec((SUBLANES, LANES), lambda i: (0, 0)),
                  pl.BlockSpec(memory_space=pl.ANY)],
        out_specs=tok,
        out_shape=jax.ShapeDtypeStruct((t, SUBLANES, LANES), F32),
        scratch_shapes=([pltpu.VMEM((PEER_PICKS, SUBLANES, 2 * LANES), F32) for _ in range(PEER_SLOTS)]
                        + [pltpu.VMEM((PEER_SLOTS, PEER_PICKS, LANES), F32),
                           pltpu.SMEM((PEER_SLOTS, PEER_PICKS), jnp.int32),
                           pltpu.SemaphoreType.DMA((PEER_SLOTS,)), pltpu.SemaphoreType.DMA((PEER_SLOTS,))]),
        compiler_params=_params("arbitrary"),
        name="peer",
    )(idx, g, _as_vregs(xn), _as_vregs(x1), _as_vregs(fw)[0], uv)
    return out.reshape(t, d)


def _forward(x, p):
    b, s, d = x.shape
    tk = min(512, s // 2)
    ch = 2 * LANES
    qat, ka, vat, qbt, kb, vbt = _prep(x, p["anw"], p["w_fused"], _position_tables(s),
                                       p["qnw"], p["qnwp"], p["knw"], p["knwp"], p["ones"], tk)
    oa = _diff_attn(qat, ka, vat, p["lq1"], p["lk1"], p["lq2"], p["lk2"], p["sw"], min(2048, s), tk, ch)
    ob = _gqa_attn(qbt, kb, vbt, min(4 * LANES, s), tk, ch)
    t = b * s
    x1, xn, sc = _mix(oa.reshape(t, A_COLS), ob.reshape(t, B_Q_COLS), x.reshape(t, d),
                      p["wo"], p["fnw"], p["wq"], p["keys"], min(512, t))
    idx, g = _topk(sc, LANES)
    y = _peer(idx, g, xn, x1, p["fw"], p["uv"], min(2 * LANES, t))
    return y.reshape(b, s, d)


def kernel(x_prompt, x_sample, attn_norm, w_in, lambda_q1, lambda_k1, lambda_q2, lambda_k2, subln_w,
           q_norm_w, k_norm_w, w_out, ffn_norm, peer_wq, peer_keys, peer_u, peer_v, final_norm):
    perm_b, _ = _rotate_half_perm(B_Q_COLS, B_DIM // 2)
    qnw = jnp.tile(q_norm_w[0], B_HEADS)
    knw = jnp.tile(k_norm_w[0], B_KV_HEADS)
    p = dict(
        anw=attn_norm[0][None, :],
        w_fused=_fused_in_weight(w_in[0]),
        qnw=qnw[None, :], qnwp=qnw[perm_b][None, :],
        knw=knw[None, :], knwp=knw[perm_b[:B_KV_COLS]][None, :],
        ones=_group_ones(B_Q_COLS, B_DIM),
        lq1=lambda_q1[0][None, :], lk1=lambda_k1[0][None, :],
        lq2=lambda_q2[0][None, :], lk2=lambda_k2[0][None, :],
        sw=subln_w[0][None, :],
        wo=w_out[0].astype(BF16), fnw=ffn_norm[0][None, :],
        wq=peer_wq[0].astype(BF16),
        keys=peer_keys[0].reshape(2 * PEER_HEADS, PEER_N_KEYS, PEER_D_HALF).astype(BF16),
        uv=jnp.concatenate([_as_vregs(peer_u[0]), _as_vregs(peer_v[0])], axis=2), fw=final_norm[None, :],
    )
    return _forward(x_prompt, p), _forward(x_sample, p)
```

```python
import functools
import math

import numpy as np
import jax
import jax.numpy as jnp
from jax import lax
from jax.experimental import pallas as pl
from jax.experimental.pallas import tpu as pltpu

F32 = jnp.float32
BF16 = jnp.bfloat16

D_MODEL = 1024
A_HEADS = 4
A_DIM = 64
A_VDIM = 2 * A_DIM
B_HEADS = 8
B_KV_HEADS = 2
B_DIM = 64
B_GROUP = B_HEADS // B_KV_HEADS
A_COLS = A_HEADS * 2 * A_DIM
B_Q_COLS = B_HEADS * B_DIM
B_KV_COLS = B_KV_HEADS * B_DIM
ROPE_THETA = 10000.0
GRID_W = 64
NORM_EPS = 1e-6
LAMBDA_INIT = 0.8 - 0.6 * math.exp(-0.3)
PEER_HEADS = 8
PEER_N_KEYS = 128
PEER_D_HALF = 128
PEER_TOPK = 16
PEER_PICKS = PEER_HEADS * PEER_TOPK

LANES = 128
SUBLANES = 8
HALF = LANES // 2
BF16_SUBLANES = 16
ONES_ROWS = BF16_SUBLANES
V_ROWS_A = A_VDIM + ONES_ROWS
V_ROWS_B = B_DIM + ONES_ROWS
VMEM_LIMIT_BYTES = 56 * 1024 * 1024

OFF_QA, OFF_QA_ROT = 0, 512
OFF_KA, OFF_KA_ROT = 1024, 1536
OFF_VA = 2048
OFF_QB, OFF_QB_ROT = 2560, 3072
OFF_KB, OFF_KB_ROT = 3584, 3712
OFF_VB = 3840
FUSED_COLS = 3968


def _params(*semantics):
    return pltpu.CompilerParams(dimension_semantics=semantics, vmem_limit_bytes=VMEM_LIMIT_BYTES)


def _rope_tables(pos, dim):
    inv = ROPE_THETA ** (-jnp.arange(0, dim, 2, dtype=F32) / dim)
    ang = pos[:, None] * inv[None, :]
    ang = jnp.concatenate([ang, ang], axis=-1)
    return jnp.cos(ang), jnp.sin(ang)


def _position_tables(s):
    rows = s // GRID_W
    t = jnp.arange(s, dtype=F32)
    row = jnp.broadcast_to(jnp.arange(rows, dtype=F32)[:, None], (rows, GRID_W)).reshape(s)
    col = jnp.broadcast_to(jnp.arange(GRID_W, dtype=F32)[None, :], (rows, GRID_W)).reshape(s)
    cos_a, sin_a = _rope_tables(t, A_DIM)
    cos_r, sin_r = _rope_tables(row, B_DIM // 2)
    cos_c, sin_c = _rope_tables(col, B_DIM // 2)
    cos_b = jnp.concatenate([cos_r, cos_c], axis=-1)
    sin_b = jnp.concatenate([sin_r, sin_c], axis=-1)
    two = lambda a: jnp.concatenate([a, a], axis=-1)
    return two(cos_a), two(sin_a), two(cos_b), two(sin_b)


def _rotate_half_perm(n_cols, block):
    i = np.arange(n_cols)
    o = i % block
    half = block // 2
    perm = np.where(o < half, i + half, i - half)
    sign = np.where(o < half, -1.0, 1.0).astype(np.float32)
    return perm, sign


def _fused_in_weight(w_in):
    wqa = w_in[:, 0:512]
    wka = w_in[:, 512:1024]
    wva = w_in[:, 1024:1536]
    wqb = w_in[:, 1536:2048]
    wkb = w_in[:, 2048:2176]
    wvb = w_in[:, 2176:2304]

    def rot(m, block):
        perm, sign = _rotate_half_perm(m.shape[1], block)
        return m[:, perm] * sign

    parts = [wqa, rot(wqa, A_DIM), wka, rot(wka, A_DIM), wva,
             wqb, rot(wqb, B_DIM // 2), wkb, rot(wkb, B_DIM // 2), wvb]
    return jnp.concatenate(parts, axis=1).astype(BF16)


def _group_ones(n, group):
    i = np.arange(n) // group
    return jnp.asarray((i[:, None] == i[None, :]).astype(np.float32), dtype=BF16)


def _rms(x):
    return x * lax.rsqrt(jnp.mean(x * x, axis=-1, keepdims=True) + NORM_EPS)


def _group_mean_sq(p, ones):
    sq = p * p
    hi = sq.astype(BF16)
    lo = (sq - hi.astype(F32)).astype(BF16)
    ss = jnp.dot(hi, ones, preferred_element_type=F32) + jnp.dot(lo, ones, preferred_element_type=F32)
    return ss * (1.0 / B_DIM)


def _prep_kernel(x_ref, anw_ref, w_ref, cosa_ref, sina_ref, cosb_ref, sinb_ref,
                 qnw_ref, qnwp_ref, knw_ref, knwp_ref, ones_ref,
                 qat_ref, ka_ref, vat_ref, qbt_ref, kb_ref, vbt_ref):
    x = x_ref[0]
    tm = x.shape[0]
    hb = (_rms(x) * anw_ref[...]).astype(BF16)

    def proj(off, n):
        return jnp.dot(hb, w_ref[:, off:off + n], preferred_element_type=F32)

    def tile_lanes(a, n):
        return jnp.concatenate([a] * n, axis=1)

    zeros_half = jnp.zeros((HALF, tm), F32)
    ones_rows = jnp.ones((ONES_ROWS, tm), BF16)

    cosa = tile_lanes(cosa_ref[...], A_COLS // LANES)
    sina = tile_lanes(sina_ref[...], A_COLS // LANES)
    p_qa, p_qa_rot = proj(OFF_QA, A_COLS), proj(OFF_QA_ROT, A_COLS)
    p_ka, p_ka_rot = proj(OFF_KA, A_COLS), proj(OFF_KA_ROT, A_COLS)
    va = proj(OFF_VA, A_COLS)
    pq, pq_rot = proj(OFF_QB, B_Q_COLS), proj(OFF_QB_ROT, B_Q_COLS)
    assert OFF_KB_ROT == OFF_KB + B_KV_COLS
    pk2 = proj(OFF_KB, 2 * B_KV_COLS)
    p_vb = proj(OFF_VB, B_KV_COLS)
    qa = (p_qa * cosa + p_qa_rot * sina) * (A_DIM ** -0.5)
    ka = p_ka * cosa + p_ka_rot * sina
    for h in range(A_HEADS):
        qt = qa[:, h * LANES:(h + 1) * LANES].T
        qat_ref[0, 2 * h] = jnp.concatenate([qt[:HALF], zeros_half], axis=0).astype(BF16)
        qat_ref[0, 2 * h + 1] = jnp.concatenate([zeros_half, qt[HALF:]], axis=0).astype(BF16)
        vat_ref[0, h, 0, :LANES, :] = va[:, h * LANES:(h + 1) * LANES].T.astype(BF16)
        vat_ref[0, h, 0, LANES:, :] = ones_rows
    ka_ref[0] = ka.astype(BF16)

    cosb = tile_lanes(cosb_ref[...], B_Q_COLS // LANES)
    sinb = tile_lanes(sinb_ref[...], B_Q_COLS // LANES)
    rq = lax.rsqrt(_group_mean_sq(pq, ones_ref[...]) + NORM_EPS)
    qn = pq * rq * qnw_ref[...]
    qn_rot = pq_rot * rq * qnwp_ref[...]
    qb = (qn * cosb + qn_rot * sinb) * (B_DIM ** -0.5)
    for c in range(B_HEADS // 2):
        qt = qb[:, c * LANES:(c + 1) * LANES].T
        for parity in range(2):
            h = 2 * c + parity
            piece = qt[parity * HALF:(parity + 1) * HALF]
            parts = [piece, zeros_half] if h // B_GROUP == 0 else [zeros_half, piece]
            qbt_ref[0, h] = jnp.concatenate(parts, axis=0).astype(BF16)

    pk = pk2[:, :B_KV_COLS]
    rk = lax.rsqrt(_group_mean_sq(pk, ones_ref[:B_KV_COLS, :B_KV_COLS]) + NORM_EPS)
    kn = pk * rk * knw_ref[...]
    kn_rot = pk2[:, B_KV_COLS:] * rk * knwp_ref[...]
    kb_ref[0] = (kn * cosb_ref[...] + kn_rot * sinb_ref[...]).astype(BF16)
    vbt = p_vb.T.astype(BF16)
    for g in range(B_KV_HEADS):
        vbt_ref[0, 0, g, :B_DIM, :] = vbt[g * B_DIM:(g + 1) * B_DIM]
        vbt_ref[0, 0, g, B_DIM:, :] = ones_rows


def _prep(x, anw, w_fused, tables, qnw, qnwp, knw, knwp, ones, tm):
    b, s, d = x.shape
    nt = s // tm
    cosa, sina, cosb, sinb = tables
    full = lambda shape: pl.BlockSpec(shape, lambda bi, i: (0,) * len(shape))
    tab = pl.BlockSpec((tm, LANES), lambda bi, i: (i, 0))
    tok = lambda n: pl.BlockSpec((1, tm, n), lambda bi, i: (bi, i, 0))
    qt = lambda heads: pl.BlockSpec((1, heads, LANES, tm), lambda bi, i: (bi, 0, 0, i))
    sds = lambda *shape: jax.ShapeDtypeStruct(shape, BF16)
    return pl.pallas_call(
        _prep_kernel,
        grid=(b, nt),
        in_specs=[pl.BlockSpec((1, tm, d), lambda bi, i: (bi, i, 0)),
                  full((1, d)), full((d, FUSED_COLS)), tab, tab, tab, tab,
                  full((1, B_Q_COLS)), full((1, B_Q_COLS)), full((1, B_KV_COLS)), full((1, B_KV_COLS)),
                  full((B_Q_COLS, B_Q_COLS))],
        out_specs=[qt(2 * A_HEADS), tok(A_COLS),
                   pl.BlockSpec((1, A_HEADS, 1, V_ROWS_A, tm), lambda bi, i: (bi, 0, i, 0, 0)),
                   qt(B_HEADS), tok(B_KV_COLS),
                   pl.BlockSpec((1, 1, B_KV_HEADS, V_ROWS_B, tm), lambda bi, i: (bi, i, 0, 0, 0))],
        out_shape=[sds(b, 2 * A_HEADS, LANES, s), sds(b, s, A_COLS), sds(b, A_HEADS, nt, V_ROWS_A, tm),
                   sds(b, B_HEADS, LANES, s), sds(b, s, B_KV_COLS), sds(b, nt, B_KV_HEADS, V_ROWS_B, tm)],
        compiler_params=_params("parallel", "parallel"),
        name="prep",
    )(x, anw, w_fused, cosa, sina, cosb, sinb, qnw, qnwp, knw, knwp, ones)


def _flash_scratch(rows, tk, v_rows):
    return [pltpu.VMEM((1, rows), F32), pltpu.VMEM((v_rows, rows), F32),
            pltpu.VMEM((tk, rows), F32), pltpu.VMEM((tk, rows), F32)]


def _flash_t(q_chunk, n_chunks, ch, k_ref, vt_tile, m_sc, acc_sc, s0_sc, s1_sc):
    m_sc[...] = jnp.full(m_sc.shape, -jnp.inf, F32)
    acc_sc[...] = jnp.zeros(acc_sc.shape, F32)
    tk = s0_sc.shape[0]
    nk = k_ref.shape[1] // tk
    assert nk >= 2 and nk % 2 == 0, "key tiles are processed in pairs"
    chunks = [slice(c * ch, (c + 1) * ch) for c in range(n_chunks)]

    def scores(j, c, dst):
        k = k_ref[0, pl.ds(pl.multiple_of(j * tk, tk), tk), :]
        dst[:, chunks[c]] = jnp.dot(k, q_chunk(c), preferred_element_type=F32)

    def softmax_pv(j, c, src):
        sl = chunks[c]
        s = src[:, sl]
        m_prev = m_sc[:, sl]
        m_new = jnp.maximum(m_prev, jnp.max(s, axis=0, keepdims=True))
        p = jnp.exp(s - m_new).astype(BF16)
        acc_sc[:, sl] = (jnp.exp(m_prev - m_new) * acc_sc[:, sl]
                         + jnp.dot(vt_tile(j, c), p, preferred_element_type=F32))
        m_sc[:, sl] = m_new

    def step(j, src, dst):
        for c in range(n_chunks):
            if dst is not None:
                scores(j + 1, c, dst)
            softmax_pv(j, c, src)

    for c in range(n_chunks):
        scores(0, c, s0_sc)

    def body(i, carry):
        step(2 * i, s0_sc, s1_sc)
        step(2 * i + 1, s1_sc, s0_sc)
        return carry

    lax.fori_loop(0, nk // 2 - 1, body, 0)
    step(nk - 2, s0_sc, s1_sc)
    step(nk - 1, s1_sc, None)
    acc = acc_sc[...]
    n_val = acc.shape[0] - ONES_ROWS
    return acc[:n_val] / acc[n_val:n_val + 1]


def _diff_attn_kernel(qt_ref, k_ref, vt_ref, lq1_ref, lk1_ref, lq2_ref, lk2_ref, sw_ref,
                      o_ref, *scratch, ch):
    tq = qt_ref.shape[3]
    per_map = tq // ch

    def q_chunk(c):
        return qt_ref[0, c // per_map, :, (c % per_map) * ch:(c % per_map + 1) * ch]

    ot = _flash_t(q_chunk, 2 * per_map, ch, k_ref, lambda j, c: vt_ref[0, 0, j], *scratch)
    lam = (jnp.exp(jnp.sum(lq1_ref[...] * lk1_ref[...], axis=1, keepdims=True))
           - jnp.exp(jnp.sum(lq2_ref[...] * lk2_ref[...], axis=1, keepdims=True)) + LAMBDA_INIT)
    d = (ot[:, :tq] - lam * ot[:, tq:]).T
    o_ref[0] = (_rms(d) * sw_ref[...] * (1.0 - LAMBDA_INIT)).astype(BF16)


def _diff_attn(qat, ka, vat, lq1, lk1, lq2, lk2, sw, tq, tk, ch):
    b, s, _ = ka.shape
    small = pl.BlockSpec((1, A_DIM), lambda bi, h, i: (0, 0))
    return pl.pallas_call(
        functools.partial(_diff_attn_kernel, ch=ch),
        grid=(b, A_HEADS, s // tq),
        in_specs=[pl.BlockSpec((1, 2, LANES, tq), lambda bi, h, i: (bi, h, 0, i)),
                  pl.BlockSpec((1, s, LANES), lambda bi, h, i: (bi, 0, h)),
                  pl.BlockSpec((1, 1, s // tk, V_ROWS_A, tk), lambda bi, h, i: (bi, h, 0, 0, 0)),
                  small, small, small, small, pl.BlockSpec((1, A_VDIM), lambda bi, h, i: (0, 0))],
        out_specs=pl.BlockSpec((1, tq, LANES), lambda bi, h, i: (bi, i, h)),
        out_shape=jax.ShapeDtypeStruct((b, s, A_COLS), BF16),
        scratch_shapes=_flash_scratch(2 * tq, tk, V_ROWS_A),
        compiler_params=_params("parallel", "parallel", "arbitrary"),
        name="diff_attn",
    )(qat, ka, vat, lq1, lk1, lq2, lk2, sw)


def _gqa_kernel(qt_ref, k_ref, vt_ref, o_ref, *scratch, ch):
    tq = qt_ref.shape[3]
    n_chunks = B_HEADS * tq // ch
    first_head = lambda c: c * ch // tq

    def q_chunk(c):
        if tq >= ch:
            off = c * ch % tq
            return qt_ref[0, first_head(c), :, off:off + ch]
        return jnp.concatenate([qt_ref[0, first_head(c) + i] for i in range(ch // tq)], axis=1)

    def vt_tile(j, c):
        return vt_ref[0, j, first_head(c) // B_GROUP]

    ot = _flash_t(q_chunk, n_chunks, ch, k_ref, vt_tile, *scratch)
    pairs = [ot[:, (2 * c) * tq:(2 * c + 2) * tq] for c in range(B_HEADS // 2)]
    o_ref[0] = jnp.concatenate(
        [jnp.concatenate([p[:, :tq], p[:, tq:]], axis=0).T for p in pairs], axis=1).astype(BF16)


def _gqa_attn(qbt, kb, vbt, tq, tk, ch):
    b, s, _ = kb.shape
    return pl.pallas_call(
        functools.partial(_gqa_kernel, ch=ch),
        grid=(b, s // tq),
        in_specs=[pl.BlockSpec((1, B_HEADS, LANES, tq), lambda bi, i: (bi, 0, 0, i)),
                  pl.BlockSpec((1, s, LANES), lambda bi, i: (bi, 0, 0)),
                  pl.BlockSpec((1, s // tk, B_KV_HEADS, V_ROWS_B, tk), lambda bi, i: (bi, 0, 0, 0, 0))],
        out_specs=pl.BlockSpec((1, tq, B_Q_COLS), lambda bi, i: (bi, i, 0)),
        out_shape=jax.ShapeDtypeStruct((b, s, B_Q_COLS), BF16),
        scratch_shapes=_flash_scratch(B_HEADS * tq, tk, V_ROWS_B),
        compiler_params=_params("parallel", "arbitrary"),
        name="gqa_attn",
    )(qbt, kb, vbt)


def _mix_kernel(oa_ref, ob_ref, x_ref, wo_ref, fnw_ref, wq_ref, keys_ref, x1_ref, xn_ref, sc_ref):
    y = (jnp.dot(oa_ref[...], wo_ref[:A_COLS, :], preferred_element_type=F32)
         + jnp.dot(ob_ref[...], wo_ref[A_COLS:, :], preferred_element_type=F32))
    x1 = x_ref[...] + y
    xn = _rms(x1) * fnw_ref[...]
    x1_ref[...] = x1
    xn_ref[...] = xn
    xb = xn.astype(BF16)
    qs = [jnp.dot(xb, wq_ref[:, h * 2 * PEER_D_HALF:(h + 1) * 2 * PEER_D_HALF],
                  preferred_element_type=F32).astype(BF16) for h in range(PEER_HEADS)]
    for h in range(PEER_HEADS):
        for c in range(2):
            sc_ref[2 * h + c] = lax.dot_general(
                keys_ref[2 * h + c], qs[h][:, c * PEER_D_HALF:(c + 1) * PEER_D_HALF], (((1,), (1,)), ((), ())),
                preferred_element_type=F32)


def _mix(oa, ob, x, wo, fnw, wq, keys, tm):
    t, d = x.shape
    nq = wq.shape[1]
    row = lambda n: pl.BlockSpec((tm, n), lambda i: (i, 0))
    full = lambda shape: pl.BlockSpec(shape, lambda i: (0,) * len(shape))
    return pl.pallas_call(
        _mix_kernel,
        grid=(t // tm,),
        in_specs=[row(A_COLS), row(B_Q_COLS), row(d), full((d, d)), full((1, d)), full((d, nq)),
                  full((2 * PEER_HEADS, PEER_N_KEYS, PEER_D_HALF))],
        out_specs=[row(d), row(d),
                   pl.BlockSpec((2 * PEER_HEADS, PEER_N_KEYS, tm), lambda i: (0, 0, i))],
        out_shape=[jax.ShapeDtypeStruct((t, d), F32), jax.ShapeDtypeStruct((t, d), F32),
                   jax.ShapeDtypeStruct((2 * PEER_HEADS, PEER_N_KEYS, t), F32)],
        compiler_params=_params("parallel"),
        name="mix",
    )(oa, ob, x, wo, fnw, wq, keys)


_NO_ROW = float(2 ** 20)


def _topk_rows(x, iota, k):
    vals, rows = [], []
    for _ in range(k):
        m = jnp.max(x, axis=0, keepdims=True)
        r = jnp.min(jnp.where(x == m, iota, _NO_ROW), axis=0, keepdims=True)
        vals.append(m)
        rows.append(r)
        x = jnp.where(iota == r, -jnp.inf, x)
    return jnp.concatenate(vals, axis=0), jnp.concatenate(rows, axis=0)


def _select_rows(table, sel):
    out = jnp.zeros_like(sel)
    for r in range(table.shape[0]):
        out = jnp.where(sel == float(r), table[r:r + 1, :], out)
    return out


def _pair_candidates(s1, s2, sub):
    lo, hi = s2[:SUBLANES], s2[SUBLANES:]
    vals, ids = [s1[0:1] + lo, s1[0:1] + hi], [sub, sub + float(SUBLANES)]
    for a in range(1, SUBLANES):
        bound = PEER_TOPK // (a + 1)
        v = s1[a:a + 1] + lo
        vals.append(v if bound >= SUBLANES else jnp.where(sub < float(bound), v, -jnp.inf))
        ids.append(sub + float(a * PEER_TOPK))
    vals.append(s1[SUBLANES:] + s2[0:1])
    ids.append((sub + float(SUBLANES)) * float(PEER_TOPK))
    return jnp.concatenate(vals, axis=0), jnp.concatenate(ids, axis=0)


def _topk_kernel(sc_ref, idx_ref, g_ref, idx_sc):
    tt = sc_ref.shape[2]
    iota_keys = lax.broadcasted_iota(jnp.int32, (PEER_N_KEYS, tt), 0).astype(F32)
    sub = lax.broadcasted_iota(jnp.int32, (SUBLANES, tt), 0).astype(F32)

    def key_stage(h):
        s1, i1 = _topk_rows(sc_ref[2 * h], iota_keys, PEER_TOPK)
        s2, i2 = _topk_rows(sc_ref[2 * h + 1], iota_keys, PEER_TOPK)
        return s1, i1, s2, i2

    def pair_stage(h, s1, i1, s2, i2):
        top_s, pos = _topk_rows(*_pair_candidates(s1, s2, sub), PEER_TOPK)
        a = jnp.floor(pos * (1.0 / PEER_TOPK))
        b = pos - a * PEER_TOPK
        expert = _select_rows(i1, a) * PEER_N_KEYS + _select_rows(i2, b)
        e = jnp.exp(top_s - top_s[0:1, :])
        off = pl.multiple_of(h * PEER_TOPK, PEER_TOPK)
        idx_sc[pl.ds(off, PEER_TOPK), :] = expert
        g_ref[pl.ds(off, PEER_TOPK), :] = e / jnp.sum(e, axis=0, keepdims=True)

    def body(h, carry):
        nxt = key_stage(h + 1)
        pair_stage(h, *carry)
        return nxt

    last = lax.fori_loop(0, PEER_HEADS - 1, body, key_stage(0))
    pair_stage(PEER_HEADS - 1, *last)
    idx_ref[...] = idx_sc[...].T.astype(jnp.int32)


def _topk(sc, tt):
    t = sc.shape[2]
    return pl.pallas_call(
        _topk_kernel,
        grid=(t // tt,),
        in_specs=[pl.BlockSpec((2 * PEER_HEADS, PEER_N_KEYS, tt), lambda i: (0, 0, i))],
        out_specs=[pl.BlockSpec((tt, PEER_PICKS), lambda i: (i, 0)),
                   pl.BlockSpec((PEER_PICKS, tt), lambda i: (0, i))],
        out_shape=[jax.ShapeDtypeStruct((t, PEER_PICKS), jnp.int32),
                   jax.ShapeDtypeStruct((PEER_PICKS, t), F32)],
        scratch_shapes=[pltpu.VMEM((PEER_PICKS, tt), F32)],
        compiler_params=_params("parallel"),
        name="topk",
    )(sc)


def _gelu(x):
    return 0.5 * x * (1.0 + lax.erf(x * (1.0 / math.sqrt(2.0))))


PEER_SLOTS = 8
PEER_BATCH = 2
PEER_AHEAD = PEER_SLOTS - PEER_BATCH


def _sublane_fold(vregs):
    sub = lax.broadcasted_iota(jnp.int32, (SUBLANES, LANES), 0)
    level, shift = list(vregs), SUBLANES // 2
    while len(level) > 1:
        keep_first = (sub & shift) == 0
        half = len(level) // 2
        level = [jnp.where(keep_first,
                           level[i] + pltpu.roll(level[i], SUBLANES - shift, axis=0),
                           level[i + half] + pltpu.roll(level[i + half], shift, axis=0))
                 for i in range(half)]
        shift //= 2
    return level[0]


def _peer_kernel(idx_ref, g_ref, xn_ref, x1_ref, fw_ref, uv_hbm, o_ref, *scratch):
    bufs, (w_sc, ids, sem, id_sem) = scratch[:PEER_SLOTS], scratch[PEER_SLOTS:]
    tb = xn_ref.shape[0]
    assert tb % PEER_SLOTS == 0 and tb >= 2 * PEER_SLOTS
    tok0 = pl.program_id(0) * tb

    def id_copy(t, slot):
        row = tok0 + jnp.minimum(t, tb - 1)
        return pltpu.make_async_copy(idx_ref.at[row], ids.at[slot], id_sem.at[slot])

    def issue(t, slot):
        for j in range(PEER_PICKS):
            pltpu.make_async_copy(uv_hbm.at[ids[slot, j]], bufs[slot].at[j], sem.at[slot]).start(priority=j % 2)
        id_copy(t + PEER_SLOTS, slot).start()

    def wait(slot):
        pltpu.make_async_copy(uv_hbm.at[pl.ds(0, PEER_PICKS)], bufs[slot], sem.at[slot]).wait()

    lane = lax.broadcasted_iota(jnp.int32, (PEER_PICKS, tb), 1)

    def compute(t, slot):
        buf = bufs[slot]
        x = xn_ref[t]
        acts = []
        for g in range(PEER_PICKS // SUBLANES):
            prods = [buf[g * SUBLANES + k, :, :LANES] * x for k in range(SUBLANES)]
            acts.append(jnp.sum(_sublane_fold(prods), axis=1, keepdims=True))
        act = jnp.concatenate(acts, axis=0)
        gate = jnp.sum(jnp.where(lane == t, g_ref[...], 0.0), axis=1, keepdims=True)
        w_sc[slot] = jnp.broadcast_to(gate * _gelu(act), (PEER_PICKS, LANES))
        partial = [jnp.zeros((SUBLANES, LANES), F32) for _ in range(4)]
        for j in range(PEER_PICKS):
            partial[j % 4] = partial[j % 4] + (jnp.broadcast_to(w_sc[slot, pl.ds(j, 1), :], (SUBLANES, LANES))
                                               * buf[j, :, LANES:])
        r = x1_ref[t] + ((partial[0] + partial[1]) + (partial[2] + partial[3]))
        ms = jnp.sum(jnp.sum(r * r, axis=1, keepdims=True), axis=0, keepdims=True) * (1.0 / D_MODEL)
        o_ref[t] = r * lax.rsqrt(ms + NORM_EPS) * fw_ref[...]

    def group(base, last):
        for s0 in range(0, PEER_SLOTS, PEER_BATCH):
            batch = range(s0, s0 + PEER_BATCH)
            ahead = [s for s in batch if not last or s + PEER_AHEAD < PEER_SLOTS]
            for s in batch:
                wait(s)
            for s in ahead:
                id_copy(0, (s + PEER_AHEAD) % PEER_SLOTS).wait()
            for s in ahead:
                issue(base + s + PEER_AHEAD, (s + PEER_AHEAD) % PEER_SLOTS)
            for s in batch:
                compute(base + s, s)

    for s in range(PEER_SLOTS):
        id_copy(s, s).start()
    for s in range(PEER_AHEAD):
        id_copy(0, s).wait()
    for s in range(PEER_AHEAD):
        issue(s, s)

    def steady(i, carry):
        group(i * PEER_SLOTS, False)
        return carry

    lax.fori_loop(0, tb // PEER_SLOTS - 1, steady, 0)
    group(tb - PEER_SLOTS, True)
    for s in range(PEER_SLOTS):
        id_copy(0, s).wait()


def _as_vregs(a):
    return a.reshape(*a.shape[:-1], SUBLANES, LANES)


def _peer(idx, g, xn, x1, fw, uv, tb):
    t, d = xn.shape
    tok = pl.BlockSpec((tb, SUBLANES, LANES), lambda i: (i, 0, 0))
    out = pl.pallas_call(
        _peer_kernel,
        grid=(t // tb,),
        in_specs=[pl.BlockSpec(memory_space=pl.ANY),
                  pl.BlockSpec((PEER_PICKS, tb), lambda i: (0, i)),
                  tok, tok, pl.BlockSpec((SUBLANES, LANES), lambda i: (0, 0)),
                  pl.BlockSpec(memory_space=pl.ANY)],
        out_specs=tok,
        out_shape=jax.ShapeDtypeStruct((t, SUBLANES, LANES), F32),
        scratch_shapes=([pltpu.VMEM((PEER_PICKS, SUBLANES, 2 * LANES), F32) for _ in range(PEER_SLOTS)]
                        + [pltpu.VMEM((PEER_SLOTS, PEER_PICKS, LANES), F32),
                           pltpu.SMEM((PEER_SLOTS, PEER_PICKS), jnp.int32),
                           pltpu.SemaphoreType.DMA((PEER_SLOTS,)), pltpu.SemaphoreType.DMA((PEER_SLOTS,))]),
        compiler_params=_params("arbitrary"),
        name="peer",
    )(idx, g, _as_vregs(xn), _as_vregs(x1), _as_vregs(fw)[0], uv)
    return out.reshape(t, d)


MXU_COLS = 2 * LANES
ATTN_QUERY_LANES = 4096


def _tiles(s, t):
    return dict(
        key=min(512, s // 2),
        chunk=MXU_COLS,
        diff_q=min(ATTN_QUERY_LANES // 2, s),
        gqa_q=min(ATTN_QUERY_LANES // B_HEADS, s),
        mix=min(512, t), topk=LANES, peer=min(2 * LANES, t))


def _forward(x, p):
    b, s, d = x.shape
    t = b * s
    n = _tiles(s, t)
    qat, ka, vat, qbt, kb, vbt = _prep(x, p["anw"], p["w_fused"], _position_tables(s),
                                       p["qnw"], p["qnwp"], p["knw"], p["knwp"], p["ones"], n["key"])
    oa = _diff_attn(qat, ka, vat, p["lq1"], p["lk1"], p["lq2"], p["lk2"], p["sw"],
                    n["diff_q"], n["key"], n["chunk"])
    ob = _gqa_attn(qbt, kb, vbt, n["gqa_q"], n["key"], n["chunk"])
    x1, xn, sc = _mix(oa.reshape(t, A_COLS), ob.reshape(t, B_Q_COLS), x.reshape(t, d),
                      p["wo"], p["fnw"], p["wq"], p["keys"], n["mix"])
    idx, g = _topk(sc, n["topk"])
    y = _peer(idx, g, xn, x1, p["fw"], p["uv"], n["peer"])
    return y.reshape(b, s, d)


def kernel(x_prompt, x_sample, attn_norm, w_in, lambda_q1, lambda_k1, lambda_q2, lambda_k2, subln_w,
           q_norm_w, k_norm_w, w_out, ffn_norm, peer_wq, peer_keys, peer_u, peer_v, final_norm):
    perm_b, _ = _rotate_half_perm(B_Q_COLS, B_DIM // 2)
    qnw = jnp.tile(q_norm_w[0], B_HEADS)
    knw = jnp.tile(k_norm_w[0], B_KV_HEADS)
    p = dict(
        anw=attn_norm[0][None, :],
        w_fused=_fused_in_weight(w_in[0]),
        qnw=qnw[None, :], qnwp=qnw[perm_b][None, :],
        knw=knw[None, :], knwp=knw[perm_b[:B_KV_COLS]][None, :],
        ones=_group_ones(B_Q_COLS, B_DIM),
        lq1=lambda_q1[0][None, :], lk1=lambda_k1[0][None, :],
        lq2=lambda_q2[0][None, :], lk2=lambda_k2[0][None, :],
        sw=subln_w[0][None, :],
        wo=w_out[0].astype(BF16), fnw=ffn_norm[0][None, :],
        wq=peer_wq[0].astype(BF16),
        keys=peer_keys[0].reshape(2 * PEER_HEADS, PEER_N_KEYS, PEER_D_HALF).astype(BF16),
        uv=jnp.concatenate([_as_vregs(peer_u[0]), _as_vregs(peer_v[0])], axis=2), fw=final_norm[None, :],
    )
    return _forward(x_prompt, p), _forward(x_sample, p)
```

```python
import functools
import math

import numpy as np
import jax
import jax.numpy as jnp
from jax import lax
from jax.experimental import pallas as pl
from jax.experimental.pallas import tpu as pltpu

F32 = jnp.float32
BF16 = jnp.bfloat16

D_MODEL = 1024
A_HEADS = 4
A_DIM = 64
A_VDIM = 2 * A_DIM
B_HEADS = 8
B_KV_HEADS = 2
B_DIM = 64
B_GROUP = B_HEADS // B_KV_HEADS
A_COLS = A_HEADS * 2 * A_DIM
B_Q_COLS = B_HEADS * B_DIM
B_KV_COLS = B_KV_HEADS * B_DIM
ROPE_THETA = 10000.0
GRID_W = 64
NORM_EPS = 1e-6
LAMBDA_INIT = 0.8 - 0.6 * math.exp(-0.3)
PEER_HEADS = 8
PEER_N_KEYS = 128
PEER_D_HALF = 128
PEER_TOPK = 16
PEER_PICKS = PEER_HEADS * PEER_TOPK

LANES = 128
SUBLANES = 8
HALF = LANES // 2
BF16_SUBLANES = 16
ONES_ROWS = BF16_SUBLANES
V_ROWS_A = A_VDIM + ONES_ROWS
V_ROWS_B = B_DIM + ONES_ROWS
VMEM_LIMIT_BYTES = 56 * 1024 * 1024

OFF_QA, OFF_QA_ROT = 0, 512
OFF_KA, OFF_KA_ROT = 1024, 1536
OFF_VA = 2048
OFF_QB, OFF_QB_ROT = 2560, 3072
OFF_KB, OFF_KB_ROT = 3584, 3712
OFF_VB = 3840
FUSED_COLS = 3968


def _params(*semantics):
    return pltpu.CompilerParams(dimension_semantics=semantics, vmem_limit_bytes=VMEM_LIMIT_BYTES)


def _rope_tables(pos, dim):
    inv = ROPE_THETA ** (-jnp.arange(0, dim, 2, dtype=F32) / dim)
    ang = pos[:, None] * inv[None, :]
    ang = jnp.concatenate([ang, ang], axis=-1)
    return jnp.cos(ang), jnp.sin(ang)


def _position_tables(s):
    rows = s // GRID_W
    t = jnp.arange(s, dtype=F32)
    row = jnp.broadcast_to(jnp.arange(rows, dtype=F32)[:, None], (rows, GRID_W)).reshape(s)
    col = jnp.broadcast_to(jnp.arange(GRID_W, dtype=F32)[None, :], (rows, GRID_W)).reshape(s)
    cos_a, sin_a = _rope_tables(t, A_DIM)
    cos_r, sin_r = _rope_tables(row, B_DIM // 2)
    cos_c, sin_c = _rope_tables(col, B_DIM // 2)
    cos_b = jnp.concatenate([cos_r, cos_c], axis=-1)
    sin_b = jnp.concatenate([sin_r, sin_c], axis=-1)
    two = lambda a: jnp.concatenate([a, a], axis=-1)
    return two(cos_a), two(sin_a), two(cos_b), two(sin_b)


def _rotate_half_perm(n_cols, block):
    i = np.arange(n_cols)
    o = i % block
    half = block // 2
    perm = np.where(o < half, i + half, i - half)
    sign = np.where(o < half, -1.0, 1.0).astype(np.float32)
    return perm, sign


def _fused_in_weight(w_in):
    wqa = w_in[:, 0:512]
    wka = w_in[:, 512:1024]
    wva = w_in[:, 1024:1536]
    wqb = w_in[:, 1536:2048]
    wkb = w_in[:, 2048:2176]
    wvb = w_in[:, 2176:2304]

    def rot(m, block):
        perm, sign = _rotate_half_perm(m.shape[1], block)
        return m[:, perm] * sign

    parts = [wqa, rot(wqa, A_DIM), wka, rot(wka, A_DIM), wva,
             wqb, rot(wqb, B_DIM // 2), wkb, rot(wkb, B_DIM // 2), wvb]
    return jnp.concatenate(parts, axis=1).astype(BF16)


def _group_ones(n, group):
    i = np.arange(n) // group
    return jnp.asarray((i[:, None] == i[None, :]).astype(np.float32), dtype=BF16)


def _rms(x):
    return x * lax.rsqrt(jnp.mean(x * x, axis=-1, keepdims=True) + NORM_EPS)


def _group_mean_sq(p, ones):
    sq = p * p
    hi = sq.astype(BF16)
    lo = (sq - hi.astype(F32)).astype(BF16)
    ss = jnp.dot(hi, ones, preferred_element_type=F32) + jnp.dot(lo, ones, preferred_element_type=F32)
    return ss * (1.0 / B_DIM)


def _prep_kernel(x_ref, anw_ref, w_ref, cosa_ref, sina_ref, cosb_ref, sinb_ref,
                 qnw_ref, qnwp_ref, knw_ref, knwp_ref, ones_ref,
                 qat_ref, ka_ref, vat_ref, qbt_ref, kb_ref, vbt_ref):
    x = x_ref[0]
    tm = x.shape[0]
    hb = (_rms(x) * anw_ref[...]).astype(BF16)

    def proj(off, n):
        return jnp.dot(hb, w_ref[:, off:off + n], preferred_element_type=F32)

    def tile_lanes(a, n):
        return jnp.concatenate([a] * n, axis=1)

    zeros_half = jnp.zeros((HALF, tm), F32)
    ones_rows = jnp.ones((ONES_ROWS, tm), BF16)

    cosa = tile_lanes(cosa_ref[...], A_COLS // LANES)
    sina = tile_lanes(sina_ref[...], A_COLS // LANES)
    p_qa, p_qa_rot = proj(OFF_QA, A_COLS), proj(OFF_QA_ROT, A_COLS)
    p_ka, p_ka_rot = proj(OFF_KA, A_COLS), proj(OFF_KA_ROT, A_COLS)
    va = proj(OFF_VA, A_COLS)
    pq, pq_rot = proj(OFF_QB, B_Q_COLS), proj(OFF_QB_ROT, B_Q_COLS)
    assert OFF_KB_ROT == OFF_KB + B_KV_COLS
    pk2 = proj(OFF_KB, 2 * B_KV_COLS)
    p_vb = proj(OFF_VB, B_KV_COLS)
    qa = (p_qa * cosa + p_qa_rot * sina) * (A_DIM ** -0.5)
    ka = p_ka * cosa + p_ka_rot * sina
    for h in range(A_HEADS):
        qt = qa[:, h * LANES:(h + 1) * LANES].T
        qat_ref[0, 2 * h] = jnp.concatenate([qt[:HALF], zeros_half], axis=0).astype(BF16)
        qat_ref[0, 2 * h + 1] = jnp.concatenate([zeros_half, qt[HALF:]], axis=0).astype(BF16)
        vat_ref[0, h, 0, :LANES, :] = va[:, h * LANES:(h + 1) * LANES].T.astype(BF16)
        vat_ref[0, h, 0, LANES:, :] = ones_rows
    ka_ref[0] = ka.astype(BF16)

    cosb = tile_lanes(cosb_ref[...], B_Q_COLS // LANES)
    sinb = tile_lanes(sinb_ref[...], B_Q_COLS // LANES)
    rq = lax.rsqrt(_group_mean_sq(pq, ones_ref[...]) + NORM_EPS)
    qn = pq * rq * qnw_ref[...]
    qn_rot = pq_rot * rq * qnwp_ref[...]
    qb = (qn * cosb + qn_rot * sinb) * (B_DIM ** -0.5)
    for c in range(B_HEADS // 2):
        qt = qb[:, c * LANES:(c + 1) * LANES].T
        for parity in range(2):
            h = 2 * c + parity
            piece = qt[parity * HALF:(parity + 1) * HALF]
            parts = [piece, zeros_half] if h // B_GROUP == 0 else [zeros_half, piece]
            qbt_ref[0, h] = jnp.concatenate(parts, axis=0).astype(BF16)

    pk = pk2[:, :B_KV_COLS]
    rk = lax.rsqrt(_group_mean_sq(pk, ones_ref[:B_KV_COLS, :B_KV_COLS]) + NORM_EPS)
    kn = pk * rk * knw_ref[...]
    kn_rot = pk2[:, B_KV_COLS:] * rk * knwp_ref[...]
    kb_ref[0] = (kn * cosb_ref[...] + kn_rot * sinb_ref[...]).astype(BF16)
    vbt = p_vb.T.astype(BF16)
    for g in range(B_KV_HEADS):
        vbt_ref[0, 0, g, :B_DIM, :] = vbt[g * B_DIM:(g + 1) * B_DIM]
        vbt_ref[0, 0, g, B_DIM:, :] = ones_rows


def _prep(x, anw, w_fused, tables, qnw, qnwp, knw, knwp, ones, tm):
    b, s, d = x.shape
    nt = s // tm
    cosa, sina, cosb, sinb = tables
    full = lambda shape: pl.BlockSpec(shape, lambda bi, i: (0,) * len(shape))
    tab = pl.BlockSpec((tm, LANES), lambda bi, i: (i, 0))
    tok = lambda n: pl.BlockSpec((1, tm, n), lambda bi, i: (bi, i, 0))
    qt = lambda heads: pl.BlockSpec((1, heads, LANES, tm), lambda bi, i: (bi, 0, 0, i))
    sds = lambda *shape: jax.ShapeDtypeStruct(shape, BF16)
    return pl.pallas_call(
        _prep_kernel,
        grid=(b, nt),
        in_specs=[pl.BlockSpec((1, tm, d), lambda bi, i: (bi, i, 0)),
                  full((1, d)), full((d, FUSED_COLS)), tab, tab, tab, tab,
                  full((1, B_Q_COLS)), full((1, B_Q_COLS)), full((1, B_KV_COLS)), full((1, B_KV_COLS)),
                  full((B_Q_COLS, B_Q_COLS))],
        out_specs=[qt(2 * A_HEADS), tok(A_COLS),
                   pl.BlockSpec((1, A_HEADS, 1, V_ROWS_A, tm), lambda bi, i: (bi, 0, i, 0, 0)),
                   qt(B_HEADS), tok(B_KV_COLS),
                   pl.BlockSpec((1, 1, B_KV_HEADS, V_ROWS_B, tm), lambda bi, i: (bi, i, 0, 0, 0))],
        out_shape=[sds(b, 2 * A_HEADS, LANES, s), sds(b, s, A_COLS), sds(b, A_HEADS, nt, V_ROWS_A, tm),
                   sds(b, B_HEADS, LANES, s), sds(b, s, B_KV_COLS), sds(b, nt, B_KV_HEADS, V_ROWS_B, tm)],
        compiler_params=_params("parallel", "parallel"),
        name="prep",
    )(x, anw, w_fused, cosa, sina, cosb, sinb, qnw, qnwp, knw, knwp, ones)


def _flash_scratch(rows, tk, v_rows):
    return [pltpu.VMEM((1, rows), F32), pltpu.VMEM((v_rows, rows), F32),
            pltpu.VMEM((tk, rows), F32), pltpu.VMEM((tk, rows), F32)]


def _flash_t(q_chunk, n_chunks, ch, k_ref, vt_tile, m_sc, acc_sc, s0_sc, s1_sc):
    m_sc[...] = jnp.full(m_sc.shape, -jnp.inf, F32)
    acc_sc[...] = jnp.zeros(acc_sc.shape, F32)
    tk = s0_sc.shape[0]
    nk = k_ref.shape[1] // tk
    assert nk >= 2 and nk % 2 == 0, "key tiles are processed in pairs"
    chunks = [slice(c * ch, (c + 1) * ch) for c in range(n_chunks)]

    def scores(j, c, dst):
        k = k_ref[0, pl.ds(pl.multiple_of(j * tk, tk), tk), :]
        dst[:, chunks[c]] = jnp.dot(k, q_chunk(c), preferred_element_type=F32)

    def softmax_pv(j, c, src):
        sl = chunks[c]
        s = src[:, sl]
        m_prev = m_sc[:, sl]
        m_new = jnp.maximum(m_prev, jnp.max(s, axis=0, keepdims=True))
        p = jnp.exp(s - m_new).astype(BF16)
        acc_sc[:, sl] = (jnp.exp(m_prev - m_new) * acc_sc[:, sl]
                         + jnp.dot(vt_tile(j, c), p, preferred_element_type=F32))
        m_sc[:, sl] = m_new

    def step(j, src, dst):
        for c in range(n_chunks):
            if dst is not None:
                scores(j + 1, c, dst)
            softmax_pv(j, c, src)

    for c in range(n_chunks):
        scores(0, c, s0_sc)

    def body(i, carry):
        step(2 * i, s0_sc, s1_sc)
        step(2 * i + 1, s1_sc, s0_sc)
        return carry

    lax.fori_loop(0, nk // 2 - 1, body, 0)
    step(nk - 2, s0_sc, s1_sc)
    step(nk - 1, s1_sc, None)
    acc = acc_sc[...]
    n_val = acc.shape[0] - ONES_ROWS
    return acc[:n_val] / acc[n_val:n_val + 1]


def _diff_attn_kernel(qt_ref, k_ref, vt_ref, lq1_ref, lk1_ref, lq2_ref, lk2_ref, sw_ref,
                      o_ref, *scratch, ch):
    tq = qt_ref.shape[3]
    per_map = tq // ch

    def q_chunk(c):
        return qt_ref[0, c // per_map, :, (c % per_map) * ch:(c % per_map + 1) * ch]

    ot = _flash_t(q_chunk, 2 * per_map, ch, k_ref, lambda j, c: vt_ref[0, 0, j], *scratch)
    lam = (jnp.exp(jnp.sum(lq1_ref[...] * lk1_ref[...], axis=1, keepdims=True))
           - jnp.exp(jnp.sum(lq2_ref[...] * lk2_ref[...], axis=1, keepdims=True)) + LAMBDA_INIT)
    d = (ot[:, :tq] - lam * ot[:, tq:]).T
    o_ref[0] = (_rms(d) * sw_ref[...] * (1.0 - LAMBDA_INIT)).astype(BF16)


def _diff_attn(qat, ka, vat, lq1, lk1, lq2, lk2, sw, tq, tk, ch):
    b, s, _ = ka.shape
    small = pl.BlockSpec((1, A_DIM), lambda bi, h, i: (0, 0))
    return pl.pallas_call(
        functools.partial(_diff_attn_kernel, ch=ch),
        grid=(b, A_HEADS, s // tq),
        in_specs=[pl.BlockSpec((1, 2, LANES, tq), lambda bi, h, i: (bi, h, 0, i)),
                  pl.BlockSpec((1, s, LANES), lambda bi, h, i: (bi, 0, h)),
                  pl.BlockSpec((1, 1, s // tk, V_ROWS_A, tk), lambda bi, h, i: (bi, h, 0, 0, 0)),
                  small, small, small, small, pl.BlockSpec((1, A_VDIM), lambda bi, h, i: (0, 0))],
        out_specs=pl.BlockSpec((1, tq, LANES), lambda bi, h, i: (bi, i, h)),
        out_shape=jax.ShapeDtypeStruct((b, s, A_COLS), BF16),
        scratch_shapes=_flash_scratch(2 * tq, tk, V_ROWS_A),
        compiler_params=_params("parallel", "parallel", "arbitrary"),
        name="diff_attn",
    )(qat, ka, vat, lq1, lk1, lq2, lk2, sw)


def _gqa_kernel(qt_ref, k_ref, vt_ref, o_ref, *scratch, ch):
    tq = qt_ref.shape[3]
    n_chunks = B_HEADS * tq // ch
    first_head = lambda c: c * ch // tq

    def q_chunk(c):
        if tq >= ch:
            off = c * ch % tq
            return qt_ref[0, first_head(c), :, off:off + ch]
        return jnp.concatenate([qt_ref[0, first_head(c) + i] for i in range(ch // tq)], axis=1)

    def vt_tile(j, c):
        return vt_ref[0, j, first_head(c) // B_GROUP]

    ot = _flash_t(q_chunk, n_chunks, ch, k_ref, vt_tile, *scratch)
    pairs = [ot[:, (2 * c) * tq:(2 * c + 2) * tq] for c in range(B_HEADS // 2)]
    o_ref[0] = jnp.concatenate(
        [jnp.concatenate([p[:, :tq], p[:, tq:]], axis=0).T for p in pairs], axis=1).astype(BF16)


def _gqa_attn(qbt, kb, vbt, tq, tk, ch):
    b, s, _ = kb.shape
    return pl.pallas_call(
        functools.partial(_gqa_kernel, ch=ch),
        grid=(b, s // tq),
        in_specs=[pl.BlockSpec((1, B_HEADS, LANES, tq), lambda bi, i: (bi, 0, 0, i)),
                  pl.BlockSpec((1, s, LANES), lambda bi, i: (bi, 0, 0)),
                  pl.BlockSpec((1, s // tk, B_KV_HEADS, V_ROWS_B, tk), lambda bi, i: (bi, 0, 0, 0, 0))],
        out_specs=pl.BlockSpec((1, tq, B_Q_COLS), lambda bi, i: (bi, i, 0)),
        out_shape=jax.ShapeDtypeStruct((b, s, B_Q_COLS), BF16),
        scratch_shapes=_flash_scratch(B_HEADS * tq, tk, V_ROWS_B),
        compiler_params=_params("parallel", "arbitrary"),
        name="gqa_attn",
    )(qbt, kb, vbt)


def _mix_kernel(oa_ref, ob_ref, x_ref, wo_ref, fnw_ref, wq_ref, keys_ref, x1_ref, xn_ref, sc_ref):
    y = (jnp.dot(oa_ref[...], wo_ref[:A_COLS, :], preferred_element_type=F32)
         + jnp.dot(ob_ref[...], wo_ref[A_COLS:, :], preferred_element_type=F32))
    x1 = x_ref[...] + y
    xn = _rms(x1) * fnw_ref[...]
    x1_ref[...] = x1
    xn_ref[...] = xn
    xb = xn.astype(BF16)
    qs = [jnp.dot(xb, wq_ref[:, h * 2 * PEER_D_HALF:(h + 1) * 2 * PEER_D_HALF],
                  preferred_element_type=F32).astype(BF16) for h in range(PEER_HEADS)]
    for h in range(PEER_HEADS):
        for c in range(2):
            sc_ref[2 * h + c] = lax.dot_general(
                keys_ref[2 * h + c], qs[h][:, c * PEER_D_HALF:(c + 1) * PEER_D_HALF], (((1,), (1,)), ((), ())),
                preferred_element_type=F32)


def _mix(oa, ob, x, wo, fnw, wq, keys, tm):
    t, d = x.shape
    nq = wq.shape[1]
    row = lambda n: pl.BlockSpec((tm, n), lambda i: (i, 0))
    full = lambda shape: pl.BlockSpec(shape, lambda i: (0,) * len(shape))
    return pl.pallas_call(
        _mix_kernel,
        grid=(t // tm,),
        in_specs=[row(A_COLS), row(B_Q_COLS), row(d), full((d, d)), full((1, d)), full((d, nq)),
                  full((2 * PEER_HEADS, PEER_N_KEYS, PEER_D_HALF))],
        out_specs=[row(d), row(d),
                   pl.BlockSpec((2 * PEER_HEADS, PEER_N_KEYS, tm), lambda i: (0, 0, i))],
        out_shape=[jax.ShapeDtypeStruct((t, d), F32), jax.ShapeDtypeStruct((t, d), F32),
                   jax.ShapeDtypeStruct((2 * PEER_HEADS, PEER_N_KEYS, t), F32)],
        compiler_params=_params("parallel"),
        name="mix",
    )(oa, ob, x, wo, fnw, wq, keys)


_NO_ROW = float(2 ** 20)


def _topk_rows(x, iota, k):
    vals, rows = [], []
    for _ in range(k):
        m = jnp.max(x, axis=0, keepdims=True)
        r = jnp.min(jnp.where(x == m, iota, _NO_ROW), axis=0, keepdims=True)
        vals.append(m)
        rows.append(r)
        x = jnp.where(iota == r, -jnp.inf, x)
    return jnp.concatenate(vals, axis=0), jnp.concatenate(rows, axis=0)


def _select_rows(table, sel):
    out = jnp.zeros_like(sel)
    for r in range(table.shape[0]):
        out = jnp.where(sel == float(r), table[r:r + 1, :], out)
    return out


def _pair_candidates(s1, s2, sub):
    lo, hi = s2[:SUBLANES], s2[SUBLANES:]
    vals, ids = [s1[0:1] + lo, s1[0:1] + hi], [sub, sub + float(SUBLANES)]
    for a in range(1, SUBLANES):
        bound = PEER_TOPK // (a + 1)
        v = s1[a:a + 1] + lo
        vals.append(v if bound >= SUBLANES else jnp.where(sub < float(bound), v, -jnp.inf))
        ids.append(sub + float(a * PEER_TOPK))
    vals.append(s1[SUBLANES:] + s2[0:1])
    ids.append((sub + float(SUBLANES)) * float(PEER_TOPK))
    return jnp.concatenate(vals, axis=0), jnp.concatenate(ids, axis=0)


def _topk_kernel(sc_ref, idx_ref, g_ref, idx_sc):
    tt = sc_ref.shape[2]
    iota_keys = lax.broadcasted_iota(jnp.int32, (PEER_N_KEYS, tt), 0).astype(F32)
    sub = lax.broadcasted_iota(jnp.int32, (SUBLANES, tt), 0).astype(F32)

    def key_stage(h):
        s1, i1 = _topk_rows(sc_ref[2 * h], iota_keys, PEER_TOPK)
        s2, i2 = _topk_rows(sc_ref[2 * h + 1], iota_keys, PEER_TOPK)
        return s1, i1, s2, i2

    def pair_stage(h, s1, i1, s2, i2):
        top_s, pos = _topk_rows(*_pair_candidates(s1, s2, sub), PEER_TOPK)
        a = jnp.floor(pos * (1.0 / PEER_TOPK))
        b = pos - a * PEER_TOPK
        expert = _select_rows(i1, a) * PEER_N_KEYS + _select_rows(i2, b)
        e = jnp.exp(top_s - top_s[0:1, :])
        off = pl.multiple_of(h * PEER_TOPK, PEER_TOPK)
        idx_sc[pl.ds(off, PEER_TOPK), :] = expert
        g_ref[pl.ds(off, PEER_TOPK), :] = e / jnp.sum(e, axis=0, keepdims=True)

    def body(h, carry):
        nxt = key_stage(h + 1)
        pair_stage(h, *carry)
        return nxt

    last = lax.fori_loop(0, PEER_HEADS - 1, body, key_stage(0))
    pair_stage(PEER_HEADS - 1, *last)
    idx_ref[...] = idx_sc[...].T.astype(jnp.int32)


def _topk(sc, tt):
    t = sc.shape[2]
    return pl.pallas_call(
        _topk_kernel,
        grid=(t // tt,),
        in_specs=[pl.BlockSpec((2 * PEER_HEADS, PEER_N_KEYS, tt), lambda i: (0, 0, i))],
        out_specs=[pl.BlockSpec((tt, PEER_PICKS), lambda i: (i, 0)),
                   pl.BlockSpec((PEER_PICKS, tt), lambda i: (0, i))],
        out_shape=[jax.ShapeDtypeStruct((t, PEER_PICKS), jnp.int32),
                   jax.ShapeDtypeStruct((PEER_PICKS, t), F32)],
        scratch_shapes=[pltpu.VMEM((PEER_PICKS, tt), F32)],
        compiler_params=_params("parallel"),
        name="topk",
    )(sc)


def _gelu(x):
    return 0.5 * x * (1.0 + lax.erf(x * (1.0 / math.sqrt(2.0))))


PEER_SLOTS = 8
PEER_BATCH = 2
PEER_AHEAD = PEER_SLOTS - PEER_BATCH


def _sublane_fold(vregs):
    sub = lax.broadcasted_iota(jnp.int32, (SUBLANES, LANES), 0)
    level, shift = list(vregs), SUBLANES // 2
    while len(level) > 1:
        keep_first = (sub & shift) == 0
        half = len(level) // 2
        level = [jnp.where(keep_first,
                           level[i] + pltpu.roll(level[i], SUBLANES - shift, axis=0),
                           level[i + half] + pltpu.roll(level[i + half], shift, axis=0))
                 for i in range(half)]
        shift //= 2
    return level[0]


def _peer_kernel(idx_ref, g_ref, xn_ref, x1_ref, fw_ref, uv_hbm, o_ref, *scratch):
    bufs, (w_sc, ids, sem, id_sem) = scratch[:PEER_SLOTS], scratch[PEER_SLOTS:]
    tb = xn_ref.shape[0]
    assert tb % PEER_SLOTS == 0 and tb >= 2 * PEER_SLOTS
    tok0 = pl.program_id(0) * tb

    def id_copy(t, slot):
        row = tok0 + jnp.minimum(t, tb - 1)
        return pltpu.make_async_copy(idx_ref.at[row], ids.at[slot], id_sem.at[slot])

    def issue(t, slot):
        for j in range(PEER_PICKS):
            pltpu.make_async_copy(uv_hbm.at[ids[slot, j]], bufs[slot].at[j], sem.at[slot]).start(priority=j % 2)
        id_copy(t + PEER_SLOTS, slot).start()

    def wait(slot):
        pltpu.make_async_copy(uv_hbm.at[pl.ds(0, PEER_PICKS)], bufs[slot], sem.at[slot]).wait()

    lane = lax.broadcasted_iota(jnp.int32, (PEER_PICKS, tb), 1)

    def compute(t, slot):
        buf = bufs[slot]
        x = xn_ref[t]
        acts = []
        for g in range(PEER_PICKS // SUBLANES):
            prods = [buf[g * SUBLANES + k, :, :LANES] * x for k in range(SUBLANES)]
            acts.append(jnp.sum(_sublane_fold(prods), axis=1, keepdims=True))
        act = jnp.concatenate(acts, axis=0)
        gate = jnp.sum(jnp.where(lane == t, g_ref[...], 0.0), axis=1, keepdims=True)
        w_sc[slot] = jnp.broadcast_to(gate * _gelu(act), (PEER_PICKS, LANES))
        partial = [jnp.zeros((SUBLANES, LANES), F32) for _ in range(4)]
        for j in range(PEER_PICKS):
            partial[j % 4] = partial[j % 4] + (jnp.broadcast_to(w_sc[slot, pl.ds(j, 1), :], (SUBLANES, LANES))
                                               * buf[j, :, LANES:])
        r = x1_ref[t] + ((partial[0] + partial[1]) + (partial[2] + partial[3]))
        ms = jnp.sum(jnp.sum(r * r, axis=1, keepdims=True), axis=0, keepdims=True) * (1.0 / D_MODEL)
        o_ref[t] = r * lax.rsqrt(ms + NORM_EPS) * fw_ref[...]

    def group(base, last):
        for s0 in range(0, PEER_SLOTS, PEER_BATCH):
            batch = range(s0, s0 + PEER_BATCH)
            ahead = [s for s in batch if not last or s + PEER_AHEAD < PEER_SLOTS]
            for s in batch:
                wait(s)
            for s in ahead:
                id_copy(0, (s + PEER_AHEAD) % PEER_SLOTS).wait()
            for s in ahead:
                issue(base + s + PEER_AHEAD, (s + PEER_AHEAD) % PEER_SLOTS)
            for s in batch:
                compute(base + s, s)

    for s in range(PEER_SLOTS):
        id_copy(s, s).start()
    for s in range(PEER_AHEAD):
        id_copy(0, s).wait()
    for s in range(PEER_AHEAD):
        issue(s, s)

    def steady(i, carry):
        group(i * PEER_SLOTS, False)
        return carry

    lax.fori_loop(0, tb // PEER_SLOTS - 1, steady, 0)
    group(tb - PEER_SLOTS, True)
    for s in range(PEER_SLOTS):
        id_copy(0, s).wait()


def _as_vregs(a):
    return a.reshape(*a.shape[:-1], SUBLANES, LANES)


def _peer(idx, g, xn, x1, fw, uv, tb):
    t, d = xn.shape
    tok = pl.BlockSpec((tb, SUBLANES, LANES), lambda i: (i, 0, 0))
    out = pl.pallas_call(
        _peer_kernel,
        grid=(t // tb,),
        in_specs=[pl.BlockSpec(memory_space=pl.ANY),
                  pl.BlockSpec((PEER_PICKS, tb), lambda i: (0, i)),
                  tok, tok, pl.BlockSpec((SUBLANES, LANES), lambda i: (0, 0)),
                  pl.BlockSpec(memory_space=pl.ANY)],
        out_specs=tok,
        out_shape=jax.ShapeDtypeStruct((t, SUBLANES, LANES), F32),
        scratch_shapes=([pltpu.VMEM((PEER_PICKS, SUBLANES, 2 * LANES), F32) for _ in range(PEER_SLOTS)]
                        + [pltpu.VMEM((PEER_SLOTS, PEER_PICKS, LANES), F32),
                           pltpu.SMEM((PEER_SLOTS, PEER_PICKS), jnp.int32),
                           pltpu.SemaphoreType.DMA((PEER_SLOTS,)), pltpu.SemaphoreType.DMA((PEER_SLOTS,))]),
        compiler_params=_params("arbitrary"),
        name="peer",
    )(idx, g, _as_vregs(xn), _as_vregs(x1), _as_vregs(fw)[0], uv)
    return out.reshape(t, d)


MXU_COLS = 2 * LANES
ATTN_QUERY_LANES = 4096


def _tiles(s, t):
    return dict(
        key=min(512, s // 2),
        chunk=MXU_COLS,
        diff_q=min(ATTN_QUERY_LANES // 2, s),
        gqa_q=min(ATTN_QUERY_LANES // B_HEADS, s),
        mix=min(512, t), topk=LANES, peer=min(4 * LANES, t))


def _forward(x, p):
    b, s, d = x.shape
    t = b * s
    n = _tiles(s, t)
    qat, ka, vat, qbt, kb, vbt = _prep(x, p["anw"], p["w_fused"], _position_tables(s),
                                       p["qnw"], p["qnwp"], p["knw"], p["knwp"], p["ones"], n["key"])
    oa = _diff_attn(qat, ka, vat, p["lq1"], p["lk1"], p["lq2"], p["lk2"], p["sw"],
                    n["diff_q"], n["key"], n["chunk"])
    ob = _gqa_attn(qbt, kb, vbt, n["gqa_q"], n["key"], n["chunk"])
    x1, xn, sc = _mix(oa.reshape(t, A_COLS), ob.reshape(t, B_Q_COLS), x.reshape(t, d),
                      p["wo"], p["fnw"], p["wq"], p["keys"], n["mix"])
    idx, g = _topk(sc, n["topk"])
    y = _peer(idx, g, xn, x1, p["fw"], p["uv"], n["peer"])
    return y.reshape(b, s, d)


def kernel(x_prompt, x_sample, attn_norm, w_in, lambda_q1, lambda_k1, lambda_q2, lambda_k2, subln_w,
           q_norm_w, k_norm_w, w_out, ffn_norm, peer_wq, peer_keys, peer_u, peer_v, final_norm):
    perm_b, _ = _rotate_half_perm(B_Q_COLS, B_DIM // 2)
    qnw = jnp.tile(q_norm_w[0], B_HEADS)
    knw = jnp.tile(k_norm_w[0], B_KV_HEADS)
    p = dict(
        anw=attn_norm[0][None, :],
        w_fused=_fused_in_weight(w_in[0]),
        qnw=qnw[None, :], qnwp=qnw[perm_b][None, :],
        knw=knw[None, :], knwp=knw[perm_b[:B_KV_COLS]][None, :],
        ones=_group_ones(B_Q_COLS, B_DIM),
        lq1=lambda_q1[0][None, :], lk1=lambda_k1[0][None, :],
        lq2=lambda_q2[0][None, :], lk2=lambda_k2[0][None, :],
        sw=subln_w[0][None, :],
        wo=w_out[0].astype(BF16), fnw=ffn_norm[0][None, :],
        wq=peer_wq[0].astype(BF16),
        keys=peer_keys[0].reshape(2 * PEER_HEADS, PEER_N_KEYS, PEER_D_HALF).astype(BF16),
        uv=jnp.concatenate([_as_vregs(peer_u[0]), _as_vregs(peer_v[0])], axis=2), fw=final_norm[None, :],
    )
    return _forward(x_prompt, p), _forward(x_sample, p)
```
